```python
import jax, jax.numpy as jnp
from jax import lax
import numpy as np

D_MODEL = 2048
BATCH = 2
SEQ = 4096
DEPTH = 4
DEC_BATCH = 4
DEC_SEQ = 8192
PAST_LEN = 128

GRID_W = 64
N_MEM = 256
N_BRANCH = 3
D_CONV = 1024
CONV_WIDTH = 31
D_SGU = 1024
SGU_CHUNK = 128
SGU_GROUPS = 8
SGU_GROUP_DIM = D_SGU // SGU_GROUPS
NA_HEADS = 16
NA_HEAD_DIM = 64
D_NA = NA_HEADS * NA_HEAD_DIM
NA_ROWS_MAX = 8
NA_COLS = 16
XA_HEADS = 4
XA_HEAD_DIM = D_MODEL // XA_HEADS
D_FF = 4 * D_MODEL
SPLITS = [D_CONV, D_CONV, D_SGU, D_SGU, D_NA, D_NA, D_NA]
D_IN = sum(SPLITS) + N_BRANCH * D_MODEL
EPS = 1e-6

kernel_name = "hybrid_conv_sgu_natten_encoder"


def rms_norm(x, g):
    xf = x.astype(jnp.float32)
    y = xf * lax.rsqrt(jnp.mean(xf * xf, axis=-1, keepdims=True) + EPS)
    return y.astype(x.dtype) * g


def layer_norm(x, g, b):
    xf = x.astype(jnp.float32)
    mu = jnp.mean(xf, axis=-1, keepdims=True)
    var = jnp.mean(jnp.square(xf - mu), axis=-1, keepdims=True)
    y = (xf - mu) * lax.rsqrt(var + EPS)
    return y.astype(x.dtype) * g + b


def neighbourhood_attention(q, k, v, rpb):
    B, T, H, dh = q.shape
    rows = T // GRID_W
    kr = min(NA_ROWS_MAX, rows)
    qg = q.reshape(B, rows, GRID_W, H, dh)
    kg = k.reshape(B, rows, GRID_W, H, dh)
    vg = v.reshape(B, rows, GRID_W, H, dh)
    cols = jnp.arange(GRID_W)
    c_start = jnp.clip(cols - NA_COLS // 2, 0, GRID_W - NA_COLS)
    col_idx = c_start[:, None] + jnp.arange(NA_COLS)[None, :]
    dc = col_idx - cols[:, None] + (NA_COLS - 1)
    scale = dh ** -0.5

    def one_row(r):
        r_start = jnp.clip(r - kr // 2, 0, rows - kr)
        k_rows = lax.dynamic_slice_in_dim(kg, r_start, kr, axis=1)
        v_rows = lax.dynamic_slice_in_dim(vg, r_start, kr, axis=1)
        k_win = k_rows[:, :, col_idx]
        v_win = v_rows[:, :, col_idx]
        q_row = lax.dynamic_index_in_dim(qg, r, axis=1, keepdims=False)
        dr = r_start + jnp.arange(kr) - r + (NA_ROWS_MAX - 1)
        bias = rpb[:, dr[:, None, None], dc[None, :, :]]
        bias = bias.transpose(0, 2, 1, 3).astype(jnp.float32)
        s = jnp.einsum('bqhd,bpqkhd->bhqpk', q_row, k_win).astype(jnp.float32) * scale + bias[None]
        p = jax.nn.softmax(s.reshape(B, H, GRID_W, kr * NA_COLS), axis=-1)
        p = p.reshape(B, H, GRID_W, kr, NA_COLS).astype(v.dtype)
        return jnp.einsum('bhqpk,bpqkhd->bqhd', p, v_win)

    out = lax.map(one_row, jnp.arange(rows))
    return out.transpose(1, 0, 2, 3, 4).reshape(B, T, H * dh)


def mixer_block(n, w_in, conv_dw, conv_b, conv_ln_g, conv_ln_b, sgu_ln_g, sgu_ln_b,
                sgu_w, sgu_b, na_rpb, w_br_conv, w_br_sgu, w_br_na, w_out):
    B, T, _ = n.shape
    z = n @ w_in
    a, bg, u, v, q, k, vv, gates = jnp.split(z, np.cumsum(SPLITS).tolist(), axis=-1)
    h = a * jax.nn.sigmoid(bg)
    h = lax.conv_general_dilated(h, conv_dw[:, None, :], (1,),
                                 [(CONV_WIDTH // 2, CONV_WIDTH // 2)],
                                 dimension_numbers=('NWC', 'WIO', 'NWC'),
                                 feature_group_count=D_CONV) + conv_b
    h = jax.nn.silu(layer_norm(h, conv_ln_g, conv_ln_b))
    o_conv = h @ w_br_conv
    vn = layer_norm(v, sgu_ln_g, sgu_ln_b).reshape(B, T // SGU_CHUNK, SGU_CHUNK, SGU_GROUPS, SGU_GROUP_DIM)
    s = jnp.einsum('gij,bnjgc->bnigc', sgu_w, vn) + sgu_b.T[None, None, :, :, None]
    o_sgu = (u * s.reshape(B, T, D_SGU)) @ w_br_sgu
    o_na = neighbourhood_attention(q.reshape(B, T, NA_HEADS, NA_HEAD_DIM),
                                   k.reshape(B, T, NA_HEADS, NA_HEAD_DIM),
                                   vv.reshape(B, T, NA_HEADS, NA_HEAD_DIM), na_rpb) @ w_br_na
    g = jax.nn.sigmoid(gates).reshape(B, T, N_BRANCH, D_MODEL)
    m = g[..., 0, :] * o_conv + g[..., 1, :] * o_sgu + g[..., 2, :] * o_na
    return m @ w_out


def cross_attention(h, mem, norm_mem, wq, wk, wv, wo):
    B, T, _ = h.shape
    M = mem.shape[1]
    mn = rms_norm(mem, norm_mem)
    q = (h @ wq).reshape(B, T, XA_HEADS, XA_HEAD_DIM)
    k = (mn @ wk).reshape(B, M, XA_HEADS, XA_HEAD_DIM)
    v = (mn @ wv).reshape(B, M, XA_HEADS, XA_HEAD_DIM)
    s = jnp.einsum('bthd,bmhd->bhtm', q, k).astype(jnp.float32) * (XA_HEAD_DIM ** -0.5)
    p = jax.nn.softmax(s, axis=-1).astype(v.dtype)
    o = jnp.einsum('bhtm,bmhd->bthd', p, v).reshape(B, T, D_MODEL)
    return o @ wo


def trunk(x, mem, norm_mix, w_in, conv_dw, conv_b, conv_ln_g, conv_ln_b, sgu_ln_g, sgu_ln_b,
          sgu_w, sgu_b, na_rpb, w_br_conv, w_br_sgu, w_br_na, w_out, norm_xa, norm_mem,
          xa_wq, xa_wk, xa_wv, xa_wo, norm_ffn, ffn_w1, ffn_w2, norm_final):
    for l in range(DEPTH):
        x = x + mixer_block(rms_norm(x, norm_mix[l]), w_in[l], conv_dw[l], conv_b[l], conv_ln_g[l],
                            conv_ln_b[l], sgu_ln_g[l], sgu_ln_b[l], sgu_w[l], sgu_b[l], na_rpb[l],
                            w_br_conv[l], w_br_sgu[l], w_br_na[l], w_out[l])
        x = x + cross_attention(rms_norm(x, norm_xa[l]), mem, norm_mem[l],
                                xa_wq[l], xa_wk[l], xa_wv[l], xa_wo[l])
        hf = jnp.square(jax.nn.relu(rms_norm(x, norm_ffn[l]) @ ffn_w1[l]))
        x = x + hf @ ffn_w2[l]
    return rms_norm(x, norm_final)


def setup_inputs(seed: int = 0) -> dict:
    key = jax.random.key(seed)
    ks = iter(jax.random.split(key, 40))
    f32 = jnp.float32

    def nrm(shape, scale):
        return jax.random.normal(next(ks), shape, f32) * scale

    def gain(shape):
        return 1.0 + 0.01 * jax.random.normal(next(ks), shape, f32)

    L, D = DEPTH, D_MODEL
    return {
        "x_prompt": nrm((BATCH, SEQ, D), 1.0),
        "x_sample": nrm((DEC_BATCH, DEC_SEQ, D), 1.0),
        "mem_prompt": nrm((BATCH, N_MEM, D), 1.0),
        "mem_sample": nrm((DEC_BATCH, N_MEM, D), 1.0),
        "norm_mix": gain((L, D)),
        "w_in": nrm((L, D, D_IN), D ** -0.5),
        "conv_dw": nrm((L, CONV_WIDTH, D_CONV), CONV_WIDTH ** -0.5),
        "conv_b": nrm((L, D_CONV), 0.01),
        "conv_ln_g": gain((L, D_CONV)),
        "conv_ln_b": nrm((L, D_CONV), 0.01),
        "sgu_ln_g": gain((L, D_SGU)),
        "sgu_ln_b": nrm((L, D_SGU), 0.01),
        "sgu_w": nrm((L, SGU_GROUPS, SGU_CHUNK, SGU_CHUNK), SGU_CHUNK ** -0.5),
        "sgu_b": gain((L, SGU_GROUPS, SGU_CHUNK)),
        "na_rpb": nrm((L, NA_HEADS, 2 * NA_ROWS_MAX - 1, 2 * NA_COLS - 1), 0.02),
        "w_br_conv": nrm((L, D_CONV, D), D_CONV ** -0.5),
        "w_br_sgu": nrm((L, D_SGU, D), D_SGU ** -0.5),
        "w_br_na": nrm((L, D_NA, D), D_NA ** -0.5),
        "w_out": nrm((L, D, D), D ** -0.5),
        "norm_xa": gain((L, D)),
        "norm_mem": gain((L, D)),
        "xa_wq": nrm((L, D, D), D ** -0.5),
        "xa_wk": nrm((L, D, D), D ** -0.5),
        "xa_wv": nrm((L, D, D), D ** -0.5),
        "xa_wo": nrm((L, D, D), D ** -0.5),
        "norm_ffn": gain((L, D)),
        "ffn_w1": nrm((L, D, D_FF), D ** -0.5),
        "ffn_w2": nrm((L, D_FF, D), D_FF ** -0.5),
        "norm_final": gain((D,)),
    }


def reference(x_prompt, x_sample, mem_prompt, mem_sample, norm_mix, w_in, conv_dw, conv_b,
              conv_ln_g, conv_ln_b, sgu_ln_g, sgu_ln_b, sgu_w, sgu_b, na_rpb, w_br_conv,
              w_br_sgu, w_br_na, w_out, norm_xa, norm_mem, xa_wq, xa_wk, xa_wv, xa_wo,
              norm_ffn, ffn_w1, ffn_w2, norm_final):
    y_prompt = trunk(x_prompt, mem_prompt, norm_mix, w_in, conv_dw, conv_b, conv_ln_g, conv_ln_b,
                     sgu_ln_g, sgu_ln_b, sgu_w, sgu_b, na_rpb, w_br_conv, w_br_sgu, w_br_na, w_out,
                     norm_xa, norm_mem, xa_wq, xa_wk, xa_wv, xa_wo, norm_ffn, ffn_w1, ffn_w2,
                     norm_final)
    y_sample = trunk(x_sample, mem_sample, norm_mix, w_in, conv_dw, conv_b, conv_ln_g, conv_ln_b,
                     sgu_ln_g, sgu_ln_b, sgu_w, sgu_b, na_rpb, w_br_conv, w_br_sgu, w_br_na, w_out,
                     norm_xa, norm_mem, xa_wq, xa_wk, xa_wv, xa_wo, norm_ffn, ffn_w1, ffn_w2,
                     norm_final)
    return (y_prompt, y_sample)
```

```python
import functools
import math

import jax
import jax.numpy as jnp
from jax import lax
from jax.experimental import pallas as pl
from jax.experimental.pallas import tpu as pltpu

F32 = jnp.float32
BF16 = jnp.bfloat16
EPS = 1e-6
MASKED = -1e30

V7X_VMEM_LIMIT_BYTES = 56 * 1024 * 1024
LANES = 128

GRID_W = 64
XA_HEADS = 4
CONV_HALO = 16
CONV_ROWS = 64
NA_BLOCK_ROWS = 8
NA_HALO_ROWS = 4


def _params(*semantics):
    return pltpu.CompilerParams(dimension_semantics=semantics,
                                vmem_limit_bytes=V7X_VMEM_LIMIT_BYTES)


def _tile(n, target, multiple=LANES):
    t = min(n, target)
    t -= t % multiple
    while t > multiple and n % t:
        t -= multiple
    assert t > 0 and n % t == 0, (n, target, multiple)
    return t


def _rms(x, g):
    ms = jnp.mean(x * x, axis=-1, keepdims=True)
    return x * lax.rsqrt(ms + EPS) * g


def _layer_norm(x, g, b):
    mu = jnp.mean(x, axis=-1, keepdims=True)
    d = x - mu
    var = jnp.mean(d * d, axis=-1, keepdims=True)
    return d * lax.rsqrt(var + EPS) * g + b


def _dot(a, b):
    return jnp.dot(a, b, preferred_element_type=F32)


def _dot_nt(a, b):
    return lax.dot_general(a, b, (((1,), (1,)), ((), ())), preferred_element_type=F32)


def _norm_matmul_kernel(x_ref, g_ref, w_ref, o_ref, n_ref):
    @pl.when(pl.program_id(1) == 0)
    def _():
        n_ref[...] = _rms(x_ref[...], g_ref[...]).astype(n_ref.dtype)

    o_ref[...] = _dot(n_ref[...], w_ref[...]).astype(o_ref.dtype)


def _norm_matmul(x, g, w, *, tm, tn):
    T, D = x.shape
    N = w.shape[1]
    return pl.pallas_call(
        _norm_matmul_kernel,
        grid=(T // tm, N // tn),
        in_specs=[pl.BlockSpec((tm, D), lambda i, j: (i, 0)),
                  pl.BlockSpec((1, D), lambda i, j: (0, 0)),
                  pl.BlockSpec((D, tn), lambda i, j: (0, j))],
        out_specs=pl.BlockSpec((tm, tn), lambda i, j: (i, j)),
        out_shape=jax.ShapeDtypeStruct((T, N), BF16),
        scratch_shapes=[pltpu.VMEM((tm, D), BF16)],
        compiler_params=_params("parallel", "arbitrary"),
        name="norm_matmul",
    )(x, g.reshape(1, D), w)


def _matmul_residual_kernel(a_ref, w_ref, x_ref, o_ref):
    o_ref[...] = x_ref[...] + _dot(a_ref[...], w_ref[...])


def _matmul_residual(a, w, x, *, tm, tn):
    T, K = a.shape
    N = w.shape[1]
    return pl.pallas_call(
        _matmul_residual_kernel,
        grid=(T // tm, N // tn),
        in_specs=[pl.BlockSpec((tm, K), lambda i, j: (i, 0)),
                  pl.BlockSpec((K, tn), lambda i, j: (0, j)),
                  pl.BlockSpec((tm, tn), lambda i, j: (i, j))],
        out_specs=pl.BlockSpec((tm, tn), lambda i, j: (i, j)),
        out_shape=jax.ShapeDtypeStruct((T, N), F32),
        compiler_params=_params("parallel", "arbitrary"),
        name="matmul_residual",
    )(a, w, x)


def _merge_kernel(hc_ref, hs_ref, hn_ref, wc_ref, ws_ref, wn_ref, g0_ref, g1_ref, g2_ref, o_ref):
    def gate(g_ref):
        return jax.nn.sigmoid(g_ref[...].astype(F32))

    m = gate(g0_ref) * _dot(hc_ref[...], wc_ref[...])
    m += gate(g1_ref) * _dot(hs_ref[...], ws_ref[...])
    m += gate(g2_ref) * _dot(hn_ref[...], wn_ref[...])
    o_ref[...] = m.astype(o_ref.dtype)


def _merge(hc, hs, hn, wc, ws, wn, z, gate_col0, *, tm, tn):
    T = hc.shape[0]
    D = wc.shape[1]
    gate_block0 = gate_col0 // tn
    blocks_per_gate = D // tn

    def h_spec(h):
        return pl.BlockSpec((tm, h.shape[1]), lambda i, j: (i, 0))

    def w_spec(w):
        return pl.BlockSpec((w.shape[0], tn), lambda i, j: (0, j))

    def g_spec(b):
        return pl.BlockSpec((tm, tn), lambda i, j: (i, gate_block0 + b * blocks_per_gate + j))

    return pl.pallas_call(
        _merge_kernel,
        grid=(T // tm, D // tn),
        in_specs=[h_spec(hc), h_spec(hs), h_spec(hn), w_spec(wc), w_spec(ws), w_spec(wn),
                  g_spec(0), g_spec(1), g_spec(2)],
        out_specs=pl.BlockSpec((tm, tn), lambda i, j: (i, j)),
        out_shape=jax.ShapeDtypeStruct((T, D), BF16),
        compiler_params=_params("parallel", "arbitrary"),
        name="branch_merge",
    )(hc, hs, hn, wc, ws, wn, z, z, z)


def _seq_position(t0, seqs):
    (n_a, len_a), (_, len_b) = seqs
    split = n_a * len_a
    in_a = t0 < split
    pos = jnp.where(in_a, lax.rem(t0, len_a), lax.rem(t0 - split, len_b))
    return pos, jnp.where(in_a, len_a, len_b)


def _conv_kernel(a_ref, b_ref, ap_ref, bp_ref, an_ref, bn_ref, w_ref, cb_ref, g_ref, beta_ref,
                 o_ref, hs_ref, y_ref, *, tt, seqs):
    width = w_ref.shape[0]
    C = a_ref.shape[1]
    first_tap = CONV_HALO - width // 2

    def glu(a, b):
        return a[...].astype(F32) * jax.nn.sigmoid(b[...].astype(F32))

    pos, seq_len = _seq_position(pl.program_id(0) * tt, seqs)
    hs_ref[0:CONV_HALO, :] = jnp.where(pos == 0, 0.0, glu(ap_ref, bp_ref))
    hs_ref[CONV_HALO:CONV_HALO + tt, :] = glu(a_ref, b_ref)
    hs_ref[CONV_HALO + tt:, :] = jnp.where(pos + tt == seq_len, 0.0, glu(an_ref, bn_ref))

    span = CONV_ROWS + 2 * CONV_HALO
    max_aligned = ((first_tap + width - 1) // 8) * 8

    def rows(r, carry):
        r0 = pl.multiple_of(r * CONV_ROWS, CONV_ROWS)
        for c in range(C // LANES):
            lanes = slice(c * LANES, (c + 1) * LANES)
            win = hs_ref[pl.ds(r0, span), lanes]
            acc = jnp.zeros((CONV_ROWS, LANES), F32)
            for phase in range(8):
                taps = [j for j in range(width) if (first_tap + j) % 8 == phase]
                if not taps:
                    continue
                shifted = win[phase:phase + CONV_ROWS + max_aligned]
                for j in taps:
                    a8 = first_tap + j - phase
                    acc = acc + w_ref[j:j + 1, lanes] * shifted[a8:a8 + CONV_ROWS]
            y_ref[pl.ds(r0, CONV_ROWS), lanes] = acc + cb_ref[:, lanes]
        return carry

    lax.fori_loop(0, tt // CONV_ROWS, rows, 0)

    y = _layer_norm(y_ref[...], g_ref[...], beta_ref[...])
    o_ref[...] = (y * jax.nn.sigmoid(y)).astype(o_ref.dtype)


def _conv_module(z, conv_dw, conv_b, ln_g, ln_b, seqs, *, tt):
    T = z.shape[0]
    width, C = conv_dw.shape
    assert width // 2 <= CONV_HALO and tt % CONV_ROWS == 0 and tt % CONV_HALO == 0
    assert 7 + ((CONV_HALO - width // 2 + width - 1) // 8) * 8 + CONV_ROWS <= CONV_ROWS + 2 * CONV_HALO
    halo_per_tile = tt // CONV_HALO
    last_halo = T // CONV_HALO - 1

    def cur(col):
        return pl.BlockSpec((tt, C), lambda i: (i, col))

    def prev(col):
        return pl.BlockSpec((CONV_HALO, C), lambda i: (jnp.maximum(i * halo_per_tile - 1, 0), col))

    def nxt(col):
        return pl.BlockSpec((CONV_HALO, C),
                            lambda i: (jnp.minimum((i + 1) * halo_per_tile, last_halo), col))

    def vec():
        return pl.BlockSpec((1, C), lambda i: (0, 0))

    return pl.pallas_call(
        functools.partial(_conv_kernel, tt=tt, seqs=seqs),
        grid=(T // tt,),
        in_specs=[cur(0), cur(1), prev(0), prev(1), nxt(0), nxt(1),
                  pl.BlockSpec((width, C), lambda i: (0, 0)), vec(), vec(), vec()],
        out_specs=pl.BlockSpec((tt, C), lambda i: (i, 0)),
        out_shape=jax.ShapeDtypeStruct((T, C), BF16),
        scratch_shapes=[pltpu.VMEM((tt + 2 * CONV_HALO, C), F32), pltpu.VMEM((tt, C), F32)],
        compiler_params=_params("parallel"),
        name="conv_module",
    )(z, z, z, z, z, z, conv_dw, conv_b.reshape(1, C), ln_g.reshape(1, C), ln_b.reshape(1, C))


def _sgu_kernel(u_ref, v_ref, g_ref, beta_ref, w_ref, bias_ref, o_ref, vn_ref):
    groups, chunk, _ = w_ref.shape
    tt, C = v_ref.shape
    gd = C // groups
    vn_ref[...] = _layer_norm(v_ref[...].astype(F32), g_ref[...], beta_ref[...]).astype(vn_ref.dtype)
    for n in range(tt // chunk):
        rows = slice(n * chunk, (n + 1) * chunk)
        for g in range(groups):
            lanes = slice(g * gd, (g + 1) * gd)
            s = _dot(w_ref[g], vn_ref[rows, lanes]) + bias_ref[g]
            o_ref[rows, lanes] = (u_ref[rows, lanes].astype(F32) * s).astype(o_ref.dtype)


def _sgu(z, ln_g, ln_b, sgu_w, sgu_bias, u_col, v_col, *, tt):
    T = z.shape[0]
    groups, chunk, _ = sgu_w.shape
    C = ln_g.shape[0]
    gd = C // groups
    assert gd % LANES == 0 and tt % chunk == 0

    def vec():
        return pl.BlockSpec((1, C), lambda i: (0, 0))

    return pl.pallas_call(
        _sgu_kernel,
        grid=(T // tt,),
        in_specs=[pl.BlockSpec((tt, C), lambda i: (i, u_col)),
                  pl.BlockSpec((tt, C), lambda i: (i, v_col)),
                  vec(), vec(),
                  pl.BlockSpec((groups, chunk, chunk), lambda i: (0, 0, 0)),
                  pl.BlockSpec((groups, chunk, gd), lambda i: (0, 0, 0))],
        out_specs=pl.BlockSpec((tt, C), lambda i: (i, 0)),
        out_shape=jax.ShapeDtypeStruct((T, C), BF16),
        scratch_shapes=[pltpu.VMEM((tt, C), BF16)],
        compiler_params=_params("parallel"),
        name="sgu",
    )(z, z, ln_g.reshape(1, C), ln_b.reshape(1, C), sgu_w, sgu_bias)


def _na_bias_table(rpb, na_cols):
    H, n_dr, _ = rpb.shape
    c = jnp.arange(GRID_W)
    c_start = jnp.clip(c - na_cols // 2, 0, GRID_W - na_cols)
    kc = jnp.arange(GRID_W)
    valid = (kc[None, :] >= c_start[:, None]) & (kc[None, :] < c_start[:, None] + na_cols)
    dc = jnp.clip(kc[None, :] - c[:, None] + (na_cols - 1), 0, 2 * na_cols - 2)
    dense = jnp.where(valid[None, None], rpb[:, :, dc].astype(F32), MASKED)
    pairs = jnp.concatenate([dense[:, :-1], dense[:, 1:]], axis=-1)
    pairs = pairs.reshape(H // 2, 2, n_dr - 1, GRID_W, 2 * GRID_W).transpose(0, 2, 1, 3, 4)
    return pairs.reshape(H // 2, n_dr - 1, 2 * GRID_W, 2 * GRID_W)


def _na_kernel(q_ref, kp_ref, kc_ref, kn_ref, vp_ref, vc_ref, vn_ref, tab_ref, o_ref,
               kbuf, vbuf, *, seqs, kr, scale):
    halo = NA_HALO_ROWS * GRID_W
    blk = NA_BLOCK_ROWS * GRID_W
    for buf, p, c, n in ((kbuf, kp_ref, kc_ref, kn_ref), (vbuf, vp_ref, vc_ref, vn_ref)):
        buf[0:halo, :] = p[...]
        buf[halo:halo + blk, :] = c[...]
        buf[halo + blk:, :] = n[...]

    n_pairs = q_ref.shape[1] // LANES
    row_seqs = tuple((n, length // GRID_W) for n, length in seqs)
    loc0, seq_rows = _seq_position(pl.program_id(0) * NA_BLOCK_ROWS, row_seqs)
    lane = lax.broadcasted_iota(jnp.int32, (GRID_W, LANES), 1)
    low = lane < GRID_W

    def row(j, carry):
        r_loc = loc0 + j
        r_start = jnp.clip(r_loc - kr // 2, 0, seq_rows - kr)
        dr0 = r_start - r_loc + (kr - 1)
        koff = pl.multiple_of((r_start - loc0 + NA_HALO_ROWS) * GRID_W, GRID_W)
        qoff = pl.multiple_of(j * GRID_W, GRID_W)
        for hp in range(n_pairs):
            lanes = slice(hp * LANES, (hp + 1) * LANES)
            q2 = q_ref[pl.ds(qoff, GRID_W), lanes].astype(F32)
            qq = jnp.concatenate([jnp.where(low, q2, 0.0), jnp.where(low, 0.0, q2)], axis=0)
            s = _dot_nt(qq.astype(BF16), kbuf[pl.ds(koff, kr * GRID_W), lanes]) * scale
            bias = jnp.concatenate([tab_ref[hp, dr0 + 2 * t] for t in range(kr // 2)], axis=1)
            s = s + bias
            m = jnp.max(s, axis=-1, keepdims=True)
            p = jnp.exp(s - m)
            l = jnp.sum(p, axis=-1, keepdims=True)
            r = _dot(p.astype(BF16), vbuf[pl.ds(koff, kr * GRID_W), lanes]) / l
            o_ref[pl.ds(qoff, GRID_W), lanes] = jnp.where(low, r[:GRID_W], r[GRID_W:]).astype(o_ref.dtype)
        return carry

    lax.fori_loop(0, NA_BLOCK_ROWS, row, 0)


def _neighbourhood_attention(z, table, seqs, q_col, k_col, v_col, width, head_dim):
    T = z.shape[0]
    kr = table.shape[1] // 2 + 1
    assert head_dim * 2 == LANES and GRID_W * 2 == LANES and kr % 2 == 0
    assert kr == NA_BLOCK_ROWS and NA_HALO_ROWS >= kr // 2
    for _, length in seqs:
        assert length % (NA_BLOCK_ROWS * GRID_W) == 0 and length // GRID_W >= kr
    blk = NA_BLOCK_ROWS * GRID_W
    halo = NA_HALO_ROWS * GRID_W
    halo_per_blk = blk // halo
    last_halo = T // halo - 1

    def cur(col):
        return pl.BlockSpec((blk, width), lambda i: (i, col))

    def prev(col):
        return pl.BlockSpec((halo, width), lambda i: (jnp.maximum(i * halo_per_blk - 1, 0), col))

    def nxt(col):
        return pl.BlockSpec((halo, width),
                            lambda i: (jnp.minimum((i + 1) * halo_per_blk, last_halo), col))

    return pl.pallas_call(
        functools.partial(_na_kernel, seqs=seqs, kr=kr, scale=head_dim ** -0.5),
        grid=(T // blk,),
        in_specs=[cur(q_col), prev(k_col), cur(k_col), nxt(k_col),
                  prev(v_col), cur(v_col), nxt(v_col),
                  pl.BlockSpec(table.shape, lambda i: (0, 0, 0, 0))],
        out_specs=pl.BlockSpec((blk, width), lambda i: (i, 0)),
        out_shape=jax.ShapeDtypeStruct((T, width), BF16),
        scratch_shapes=[pltpu.VMEM((blk + 2 * halo, width), BF16),
                        pltpu.VMEM((blk + 2 * halo, width), BF16)],
        compiler_params=_params("parallel"),
        name="neighbourhood_attention",
    )(z, z, z, z, z, z, z, table)


def _xattn_kernel(q_ref, k_ref, v_ref, o_ref, *, scale):
    hd = q_ref.shape[1] // XA_HEADS
    for h in range(XA_HEADS):
        lanes = slice(h * hd, (h + 1) * hd)
        s = _dot_nt(q_ref[:, lanes], k_ref[0, :, lanes]) * scale
        m = jnp.max(s, axis=-1, keepdims=True)
        p = jnp.exp(s - m)
        l = jnp.sum(p, axis=-1, keepdims=True)
        o_ref[:, lanes] = (_dot(p.astype(BF16), v_ref[0, :, lanes]) / l).astype(o_ref.dtype)


def _cross_attention(q, kv, seqs, *, tm):
    T, D = q.shape
    n_mem = kv.shape[1]
    (n_a, len_a), (_, len_b) = seqs
    assert len_a % tm == 0 and len_b % tm == 0 and (D // XA_HEADS) % LANES == 0
    split = n_a * len_a

    def seq_of(i):
        t0 = i * tm
        return jnp.where(t0 < split, t0 // len_a, n_a + (t0 - split) // len_b)

    return pl.pallas_call(
        functools.partial(_xattn_kernel, scale=(D // XA_HEADS) ** -0.5),
        grid=(T // tm,),
        in_specs=[pl.BlockSpec((tm, D), lambda i: (i, 0)),
                  pl.BlockSpec((1, n_mem, D), lambda i: (seq_of(i), 0, 0)),
                  pl.BlockSpec((1, n_mem, D), lambda i: (seq_of(i), 0, 1))],
        out_specs=pl.BlockSpec((tm, D), lambda i: (i, 0)),
        out_shape=jax.ShapeDtypeStruct((T, D), BF16),
        compiler_params=_params("parallel"),
        name="cross_attention",
    )(q, kv, kv)


def _ffn_kernel(x_ref, g_ref, w1_ref, w2_ref, o_ref, n_ref):
    @pl.when(pl.program_id(1) == 0)
    def _():
        x = x_ref[...]
        n_ref[...] = _rms(x, g_ref[...]).astype(n_ref.dtype)
        o_ref[...] = x

    h = jnp.maximum(_dot(n_ref[...], w1_ref[...]), 0.0)
    o_ref[...] += _dot((h * h).astype(BF16), w2_ref[...])


def _ffn(x, g, w1, w2, *, tm, tf):
    T, D = x.shape
    F = w1.shape[1]
    return pl.pallas_call(
        _ffn_kernel,
        grid=(T // tm, F // tf),
        in_specs=[pl.BlockSpec((tm, D), lambda i, f: (i, 0)),
                  pl.BlockSpec((1, D), lambda i, f: (0, 0)),
                  pl.BlockSpec((D, tf), lambda i, f: (0, f)),
                  pl.BlockSpec((tf, D), lambda i, f: (f, 0))],
        out_specs=pl.BlockSpec((tm, D), lambda i, f: (i, 0)),
        out_shape=jax.ShapeDtypeStruct((T, D), F32),
        scratch_shapes=[pltpu.VMEM((tm, D), BF16)],
        compiler_params=_params("parallel", "arbitrary"),
        name="ffn",
    )(x, g.reshape(1, D), w1, w2)


def _rmsnorm_kernel(x_ref, g_ref, o_ref):
    o_ref[...] = _rms(x_ref[...], g_ref[...])


def _rmsnorm(x, g, *, tm):
    T, D = x.shape
    return pl.pallas_call(
        _rmsnorm_kernel,
        grid=(T // tm,),
        in_specs=[pl.BlockSpec((tm, D), lambda i: (i, 0)), pl.BlockSpec((1, D), lambda i: (0, 0))],
        out_specs=pl.BlockSpec((tm, D), lambda i: (i, 0)),
        out_shape=jax.ShapeDtypeStruct((T, D), F32),
        compiler_params=_params("parallel"),
        name="final_rmsnorm",
    )(x, g.reshape(1, D))


def _tiles(T, seqs, sgu_chunk):
    seq_gcd = math.gcd(*[length for _, length in seqs])
    return dict(
        proj=_tile(T, 1024),
        res=_tile(T, 1024),
        merge=_tile(T, 1024),
        conv=_tile(seq_gcd, 512, CONV_ROWS),
        sgu=_tile(T, 512, sgu_chunk),
        xattn=_tile(seq_gcd, 512),
        ffn=_tile(T, 512),
    )


def kernel(x_prompt, x_sample, mem_prompt, mem_sample, norm_mix, w_in, conv_dw, conv_b, conv_ln_g, conv_ln_b, sgu_ln_g, sgu_ln_b, sgu_w, sgu_b, na_rpb, w_br_conv, w_br_sgu, w_br_na, w_out, norm_xa, norm_mem, xa_wq, xa_wk, xa_wv, xa_wo, norm_ffn, ffn_w1, ffn_w2, norm_final):
    depth, D, d_in = w_in.shape
    Bp, Sp, _ = x_prompt.shape
    Bs, Ss, _ = x_sample.shape
    seqs = ((Bp, Sp), (Bs, Ss))
    T = Bp * Sp + Bs * Ss
    d_conv = conv_dw.shape[-1]
    d_sgu = sgu_ln_g.shape[-1]
    d_na = w_br_na.shape[1]
    na_heads = na_rpb.shape[1]
    na_cols = (na_rpb.shape[3] + 1) // 2
    d_ff = ffn_w1.shape[-1]
    n_mem = mem_prompt.shape[1]
    assert d_conv == d_sgu == d_na, "column blocks of the input projection share one width"
    width = d_conv
    gate_col0 = 7 * width
    assert d_in == gate_col0 + 3 * D and width % LANES == 0

    tiles = _tiles(T, seqs, sgu_w.shape[2])
    tn_in = _tile(d_in, 1024)
    tn_d = _tile(D, 1024)
    assert gate_col0 % tn_d == 0

    x = jnp.concatenate([x_prompt.reshape(Bp * Sp, D), x_sample.reshape(Bs * Ss, D)], axis=0)
    mem = jnp.concatenate([mem_prompt.reshape(Bp * n_mem, D), mem_sample.reshape(Bs * n_mem, D)], axis=0)
    tm_mem = _tile(mem.shape[0], 512)

    for l in range(depth):
        z = _norm_matmul(x, norm_mix[l], w_in[l].astype(BF16), tm=tiles["proj"], tn=tn_in)
        h_conv = _conv_module(z, conv_dw[l], conv_b[l], conv_ln_g[l], conv_ln_b[l], seqs, tt=tiles["conv"])
        sgu_bias = jnp.broadcast_to(sgu_b[l][:, :, None], sgu_b.shape[1:] + (d_sgu // sgu_w.shape[1],))
        h_sgu = _sgu(z, sgu_ln_g[l], sgu_ln_b[l], sgu_w[l].astype(BF16), sgu_bias, 2, 3, tt=tiles["sgu"])
        h_na = _neighbourhood_attention(z, _na_bias_table(na_rpb[l], na_cols), seqs, 4, 5, 6,
                                        width, d_na // na_heads)
        m = _merge(h_conv, h_sgu, h_na, w_br_conv[l].astype(BF16), w_br_sgu[l].astype(BF16),
                   w_br_na[l].astype(BF16), z, gate_col0, tm=tiles["merge"], tn=tn_d)
        x = _matmul_residual(m, w_out[l].astype(BF16), x, tm=tiles["res"], tn=tn_d)
        q = _norm_matmul(x, norm_xa[l], xa_wq[l].astype(BF16), tm=tiles["proj"], tn=tn_d)
        w_kv = jnp.concatenate([xa_wk[l], xa_wv[l]], axis=1).astype(BF16)
        kv = _norm_matmul(mem, norm_mem[l], w_kv, tm=tm_mem, tn=tn_d).reshape(Bp + Bs, n_mem, 2 * D)
        o = _cross_attention(q, kv, seqs, tm=tiles["xattn"])
        x = _matmul_residual(o, xa_wo[l].astype(BF16), x, tm=tiles["res"], tn=tn_d)
        x = _ffn(x, norm_ffn[l], ffn_w1[l].astype(BF16), ffn_w2[l].astype(BF16),
                 tm=tiles["ffn"], tf=_tile(d_ff, 1024))

    y = _rmsnorm(x, norm_final, tm=tiles["proj"])
    y_prompt = y[:Bp * Sp].reshape(Bp, Sp, D)
    y_sample = y[Bp * Sp:].reshape(Bs, Ss, D)
    return (y_prompt, y_sample)
```

```python
import functools
import math

import jax
import jax.numpy as jnp
from jax import lax
from jax.experimental import pallas as pl
from jax.experimental.pallas import tpu as pltpu

F32 = jnp.float32
BF16 = jnp.bfloat16
EPS = 1e-6
MASKED = -1e30

V7X_VMEM_LIMIT_BYTES = 56 * 1024 * 1024
LANES = 128

GRID_W = 64
XA_HEADS = 4
CONV_HALO = 16
CONV_ROWS = 64
NA_BLOCK_ROWS = 8
NA_HALO_ROWS = 4


def _params(*semantics):
    return pltpu.CompilerParams(dimension_semantics=semantics,
                                vmem_limit_bytes=V7X_VMEM_LIMIT_BYTES)


def _tile(n, target, multiple=LANES):
    t = min(n, target)
    t -= t % multiple
    while t > multiple and n % t:
        t -= multiple
    assert t > 0 and n % t == 0, (n, target, multiple)
    return t


def _rms(x, g):
    ms = jnp.mean(x * x, axis=-1, keepdims=True)
    return x * lax.rsqrt(ms + EPS) * g


def _layer_norm(x, g, b):
    mu = jnp.mean(x, axis=-1, keepdims=True)
    d = x - mu
    var = jnp.mean(d * d, axis=-1, keepdims=True)
    return d * lax.rsqrt(var + EPS) * g + b


def _dot(a, b):
    return jnp.dot(a, b, preferred_element_type=F32)


def _dot_nt(a, b):
    return lax.dot_general(a, b, (((1,), (1,)), ((), ())), preferred_element_type=F32)


def _norm_matmul_kernel(x_ref, g_ref, w_ref, o_ref, n_ref):
    @pl.when(pl.program_id(1) == 0)
    def _():
        n_ref[...] = _rms(x_ref[...], g_ref[...]).astype(n_ref.dtype)

    o_ref[...] = _dot(n_ref[...], w_ref[...]).astype(o_ref.dtype)


def _norm_matmul(x, g, w, *, tm, tn):
    T, D = x.shape
    N = w.shape[1]
    return pl.pallas_call(
        _norm_matmul_kernel,
        grid=(T // tm, N // tn),
        in_specs=[pl.BlockSpec((tm, D), lambda i, j: (i, 0)),
                  pl.BlockSpec((1, D), lambda i, j: (0, 0)),
                  pl.BlockSpec((D, tn), lambda i, j: (0, j))],
        out_specs=pl.BlockSpec((tm, tn), lambda i, j: (i, j)),
        out_shape=jax.ShapeDtypeStruct((T, N), BF16),
        scratch_shapes=[pltpu.VMEM((tm, D), BF16)],
        compiler_params=_params("parallel", "arbitrary"),
        name="norm_matmul",
    )(x, g.reshape(1, D), w)


def _matmul_residual_kernel(a_ref, w_ref, x_ref, o_ref):
    o_ref[...] = x_ref[...] + _dot(a_ref[...], w_ref[...])


def _matmul_residual(a, w, x, *, tm, tn):
    T, K = a.shape
    N = w.shape[1]
    return pl.pallas_call(
        _matmul_residual_kernel,
        grid=(T // tm, N // tn),
        in_specs=[pl.BlockSpec((tm, K), lambda i, j: (i, 0)),
                  pl.BlockSpec((K, tn), lambda i, j: (0, j)),
                  pl.BlockSpec((tm, tn), lambda i, j: (i, j))],
        out_specs=pl.BlockSpec((tm, tn), lambda i, j: (i, j)),
        out_shape=jax.ShapeDtypeStruct((T, N), F32),
        compiler_params=_params("parallel", "arbitrary"),
        name="matmul_residual",
    )(a, w, x)


def _merge_kernel(hc_ref, hs_ref, hn_ref, wc_ref, ws_ref, wn_ref, g0_ref, g1_ref, g2_ref, o_ref):
    def gate(g_ref):
        return jax.nn.sigmoid(g_ref[...].astype(F32))

    m = gate(g0_ref) * _dot(hc_ref[...], wc_ref[...])
    m += gate(g1_ref) * _dot(hs_ref[...], ws_ref[...])
    m += gate(g2_ref) * _dot(hn_ref[...], wn_ref[...])
    o_ref[...] = m.astype(o_ref.dtype)


def _merge(hc, hs, hn, wc, ws, wn, z, gate_col0, *, tm, tn):
    T = hc.shape[0]
    D = wc.shape[1]
    gate_block0 = gate_col0 // tn
    blocks_per_gate = D // tn

    def h_spec(h):
        return pl.BlockSpec((tm, h.shape[1]), lambda i, j: (i, 0))

    def w_spec(w):
        return pl.BlockSpec((w.shape[0], tn), lambda i, j: (0, j))

    def g_spec(b):
        return pl.BlockSpec((tm, tn), lambda i, j: (i, gate_block0 + b * blocks_per_gate + j))

    return pl.pallas_call(
        _merge_kernel,
        grid=(T // tm, D // tn),
        in_specs=[h_spec(hc), h_spec(hs), h_spec(hn), w_spec(wc), w_spec(ws), w_spec(wn),
                  g_spec(0), g_spec(1), g_spec(2)],
        out_specs=pl.BlockSpec((tm, tn), lambda i, j: (i, j)),
        out_shape=jax.ShapeDtypeStruct((T, D), BF16),
        compiler_params=_params("parallel", "arbitrary"),
        name="branch_merge",
    )(hc, hs, hn, wc, ws, wn, z, z, z)


def _seq_position(t0, seqs):
    (n_a, len_a), (_, len_b) = seqs
    split = n_a * len_a
    in_a = t0 < split
    pos = jnp.where(in_a, lax.rem(t0, len_a), lax.rem(t0 - split, len_b))
    return pos, jnp.where(in_a, len_a, len_b)


def _conv_kernel(a_ref, b_ref, ap_ref, bp_ref, an_ref, bn_ref, w_ref, cb_ref, g_ref, beta_ref,
                 o_ref, hs_ref, y_ref, *, tt, seqs):
    width = w_ref.shape[0]
    C = a_ref.shape[1]
    first_tap = CONV_HALO - width // 2

    def glu(a, b):
        return a[...].astype(F32) * jax.nn.sigmoid(b[...].astype(F32))

    pos, seq_len = _seq_position(pl.program_id(0) * tt, seqs)
    hs_ref[0:CONV_HALO, :] = jnp.where(pos == 0, 0.0, glu(ap_ref, bp_ref))
    hs_ref[CONV_HALO:CONV_HALO + tt, :] = glu(a_ref, b_ref)
    hs_ref[CONV_HALO + tt:, :] = jnp.where(pos + tt == seq_len, 0.0, glu(an_ref, bn_ref))

    span = CONV_ROWS + 2 * CONV_HALO

    def rows(r, carry):
        r0 = pl.multiple_of(r * CONV_ROWS, CONV_ROWS)
        for c in range(C // LANES):
            lanes = slice(c * LANES, (c + 1) * LANES)
            win = hs_ref[pl.ds(r0, span), lanes]
            acc = jnp.zeros((CONV_ROWS, LANES), F32)
            for phase in range(8):
                taps = [j for j in range(width) if (first_tap + j) % 8 == phase]
                if not taps:
                    continue
                shifted = win if phase == 0 else pltpu.roll(win, span - phase, axis=0)
                for j in taps:
                    a8 = first_tap + j - phase
                    acc = acc + w_ref[j:j + 1, lanes] * shifted[a8:a8 + CONV_ROWS]
            y_ref[pl.ds(r0, CONV_ROWS), lanes] = acc + cb_ref[:, lanes]
        return carry

    lax.fori_loop(0, tt // CONV_ROWS, rows, 0)

    y = _layer_norm(y_ref[...], g_ref[...], beta_ref[...])
    o_ref[...] = (y * jax.nn.sigmoid(y)).astype(o_ref.dtype)


def _conv_module(z, conv_dw, conv_b, ln_g, ln_b, seqs, *, tt):
    T = z.shape[0]
    width, C = conv_dw.shape
    assert width // 2 <= CONV_HALO and tt % CONV_ROWS == 0 and tt % CONV_HALO == 0
    assert 7 + ((CONV_HALO - width // 2 + width - 1) // 8) * 8 + CONV_ROWS <= CONV_ROWS + 2 * CONV_HALO
    halo_per_tile = tt // CONV_HALO
    last_halo = T // CONV_HALO - 1

    def cur(col):
        return pl.BlockSpec((tt, C), lambda i: (i, col))

    def prev(col):
        return pl.BlockSpec((CONV_HALO, C), lambda i: (jnp.maximum(i * halo_per_tile - 1, 0), col))

    def nxt(col):
        return pl.BlockSpec((CONV_HALO, C),
                            lambda i: (jnp.minimum((i + 1) * halo_per_tile, last_halo), col))

    def vec():
        return pl.BlockSpec((1, C), lambda i: (0, 0))

    return pl.pallas_call(
        functools.partial(_conv_kernel, tt=tt, seqs=seqs),
        grid=(T // tt,),
        in_specs=[cur(0), cur(1), prev(0), prev(1), nxt(0), nxt(1),
                  pl.BlockSpec((width, C), lambda i: (0, 0)), vec(), vec(), vec()],
        out_specs=pl.BlockSpec((tt, C), lambda i: (i, 0)),
        out_shape=jax.ShapeDtypeStruct((T, C), BF16),
        scratch_shapes=[pltpu.VMEM((tt + 2 * CONV_HALO, C), F32), pltpu.VMEM((tt, C), F32)],
        compiler_params=_params("parallel"),
        name="conv_module",
    )(z, z, z, z, z, z, conv_dw, conv_b.reshape(1, C), ln_g.reshape(1, C), ln_b.reshape(1, C))


def _sgu_kernel(u_ref, v_ref, g_ref, beta_ref, w_ref, bias_ref, o_ref, vn_ref):
    groups, chunk, _ = w_ref.shape
    tt, C = v_ref.shape
    gd = C // groups
    vn_ref[...] = _layer_norm(v_ref[...].astype(F32), g_ref[...], beta_ref[...]).astype(vn_ref.dtype)
    for n in range(tt // chunk):
        rows = slice(n * chunk, (n + 1) * chunk)
        for g in range(groups):
            lanes = slice(g * gd, (g + 1) * gd)
            s = _dot(w_ref[g], vn_ref[rows, lanes]) + bias_ref[g]
            o_ref[rows, lanes] = (u_ref[rows, lanes].astype(F32) * s).astype(o_ref.dtype)


def _sgu(z, ln_g, ln_b, sgu_w, sgu_bias, u_col, v_col, *, tt):
    T = z.shape[0]
    groups, chunk, _ = sgu_w.shape
    C = ln_g.shape[0]
    gd = C // groups
    assert gd % LANES == 0 and tt % chunk == 0

    def vec():
        return pl.BlockSpec((1, C), lambda i: (0, 0))

    return pl.pallas_call(
        _sgu_kernel,
        grid=(T // tt,),
        in_specs=[pl.BlockSpec((tt, C), lambda i: (i, u_col)),
                  pl.BlockSpec((tt, C), lambda i: (i, v_col)),
                  vec(), vec(),
                  pl.BlockSpec((groups, chunk, chunk), lambda i: (0, 0, 0)),
                  pl.BlockSpec((groups, chunk, gd), lambda i: (0, 0, 0))],
        out_specs=pl.BlockSpec((tt, C), lambda i: (i, 0)),
        out_shape=jax.ShapeDtypeStruct((T, C), BF16),
        scratch_shapes=[pltpu.VMEM((tt, C), BF16)],
        compiler_params=_params("parallel"),
        name="sgu",
    )(z, z, ln_g.reshape(1, C), ln_b.reshape(1, C), sgu_w, sgu_bias)


def _na_bias_table(rpb, na_cols):
    H, n_dr, _ = rpb.shape
    c = jnp.arange(GRID_W)
    c_start = jnp.clip(c - na_cols // 2, 0, GRID_W - na_cols)
    kc = jnp.arange(GRID_W)
    valid = (kc[None, :] >= c_start[:, None]) & (kc[None, :] < c_start[:, None] + na_cols)
    dc = jnp.clip(kc[None, :] - c[:, None] + (na_cols - 1), 0, 2 * na_cols - 2)
    dense = jnp.where(valid[None, None], rpb[:, :, dc].astype(F32), MASKED)
    pairs = jnp.concatenate([dense[:, :-1], dense[:, 1:]], axis=-1)
    pairs = pairs.reshape(H // 2, 2, n_dr - 1, GRID_W, 2 * GRID_W).transpose(0, 2, 1, 3, 4)
    return pairs.reshape(H // 2, n_dr - 1, 2 * GRID_W, 2 * GRID_W)


def _na_kernel(q_ref, kp_ref, kc_ref, kn_ref, vp_ref, vc_ref, vn_ref, tab_ref, o_ref,
               kbuf, vbuf, *, seqs, kr, scale):
    halo = NA_HALO_ROWS * GRID_W
    blk = NA_BLOCK_ROWS * GRID_W
    for buf, p, c, n in ((kbuf, kp_ref, kc_ref, kn_ref), (vbuf, vp_ref, vc_ref, vn_ref)):
        buf[0:halo, :] = p[...]
        buf[halo:halo + blk, :] = c[...]
        buf[halo + blk:, :] = n[...]

    n_pairs = q_ref.shape[1] // LANES
    row_seqs = tuple((n, length // GRID_W) for n, length in seqs)
    loc0, seq_rows = _seq_position(pl.program_id(0) * NA_BLOCK_ROWS, row_seqs)
    lane = lax.broadcasted_iota(jnp.int32, (GRID_W, LANES), 1)
    low = lane < GRID_W

    def row(j, carry):
        r_loc = loc0 + j
        r_start = jnp.clip(r_loc - kr // 2, 0, seq_rows - kr)
        dr0 = r_start - r_loc + (kr - 1)
        koff = pl.multiple_of((r_start - loc0 + NA_HALO_ROWS) * GRID_W, GRID_W)
        qoff = pl.multiple_of(j * GRID_W, GRID_W)
        pair_lanes = [slice(hp * LANES, (hp + 1) * LANES) for hp in range(n_pairs)]
        scores = []
        for lanes in pair_lanes:
            q2 = q_ref[pl.ds(qoff, GRID_W), lanes].astype(F32) * scale
            qq = jnp.concatenate([jnp.where(low, q2, 0.0), jnp.where(low, 0.0, q2)], axis=0)
            scores.append(_dot_nt(qq.astype(BF16), kbuf[pl.ds(koff, kr * GRID_W), lanes]))
        probs, inv_sums = [], []
        for hp, s in enumerate(scores):
            s = s + jnp.concatenate([tab_ref[hp, dr0 + 2 * t] for t in range(kr // 2)], axis=1)
            p = jnp.exp(s - jnp.max(s, axis=-1, keepdims=True))
            inv_sums.append(1.0 / jnp.sum(p, axis=-1, keepdims=True))
            probs.append(p.astype(BF16))
        for lanes, p, inv in zip(pair_lanes, probs, inv_sums):
            r = _dot(p, vbuf[pl.ds(koff, kr * GRID_W), lanes]) * inv
            o_ref[pl.ds(qoff, GRID_W), lanes] = jnp.where(low, r[:GRID_W], r[GRID_W:]).astype(o_ref.dtype)
        return carry

    lax.fori_loop(0, NA_BLOCK_ROWS, row, 0)


def _neighbourhood_attention(z, table, seqs, q_col, k_col, v_col, width, head_dim):
    T = z.shape[0]
    kr = table.shape[1] // 2 + 1
    assert head_dim * 2 == LANES and GRID_W * 2 == LANES and kr % 2 == 0
    assert kr == NA_BLOCK_ROWS and NA_HALO_ROWS >= kr // 2
    for _, length in seqs:
        assert length % (NA_BLOCK_ROWS * GRID_W) == 0 and length // GRID_W >= kr
    blk = NA_BLOCK_ROWS * GRID_W
    halo = NA_HALO_ROWS * GRID_W
    halo_per_blk = blk // halo
    last_halo = T // halo - 1

    def cur(col):
        return pl.BlockSpec((blk, width), lambda i: (i, col))

    def prev(col):
        return pl.BlockSpec((halo, width), lambda i: (jnp.maximum(i * halo_per_blk - 1, 0), col))

    def nxt(col):
        return pl.BlockSpec((halo, width),
                            lambda i: (jnp.minimum((i + 1) * halo_per_blk, last_halo), col))

    return pl.pallas_call(
        functools.partial(_na_kernel, seqs=seqs, kr=kr, scale=head_dim ** -0.5),
        grid=(T // blk,),
        in_specs=[cur(q_col), prev(k_col), cur(k_col), nxt(k_col),
                  prev(v_col), cur(v_col), nxt(v_col),
                  pl.BlockSpec(table.shape, lambda i: (0, 0, 0, 0))],
        out_specs=pl.BlockSpec((blk, width), lambda i: (i, 0)),
        out_shape=jax.ShapeDtypeStruct((T, width), BF16),
        scratch_shapes=[pltpu.VMEM((blk + 2 * halo, width), BF16),
                        pltpu.VMEM((blk + 2 * halo, width), BF16)],
        compiler_params=_params("parallel"),
        name="neighbourhood_attention",
    )(z, z, z, z, z, z, z, table)


def _xattn_kernel(q_ref, k_ref, v_ref, o_ref, *, scale):
    hd = q_ref.shape[1] // XA_HEADS
    for h in range(XA_HEADS):
        lanes = slice(h * hd, (h + 1) * hd)
        s = _dot_nt(q_ref[:, lanes], k_ref[0, :, lanes]) * scale
        m = jnp.max(s, axis=-1, keepdims=True)
        p = jnp.exp(s - m)
        l = jnp.sum(p, axis=-1, keepdims=True)
        o_ref[:, lanes] = (_dot(p.astype(BF16), v_ref[0, :, lanes]) / l).astype(o_ref.dtype)


def _cross_attention(q, kv, seqs, *, tm):
    T, D = q.shape
    n_mem = kv.shape[1]
    (n_a, len_a), (_, len_b) = seqs
    assert len_a % tm == 0 and len_b % tm == 0 and (D // XA_HEADS) % LANES == 0
    split = n_a * len_a

    def seq_of(i):
        t0 = i * tm
        return jnp.where(t0 < split, t0 // len_a, n_a + (t0 - split) // len_b)

    return pl.pallas_call(
        functools.partial(_xattn_kernel, scale=(D // XA_HEADS) ** -0.5),
        grid=(T // tm,),
        in_specs=[pl.BlockSpec((tm, D), lambda i: (i, 0)),
                  pl.BlockSpec((1, n_mem, D), lambda i: (seq_of(i), 0, 0)),
                  pl.BlockSpec((1, n_mem, D), lambda i: (seq_of(i), 0, 1))],
        out_specs=pl.BlockSpec((tm, D), lambda i: (i, 0)),
        out_shape=jax.ShapeDtypeStruct((T, D), BF16),
        compiler_params=_params("parallel"),
        name="cross_attention",
    )(q, kv, kv)


def _ffn_kernel(x_ref, g_ref, w1_ref, w2_ref, o_ref, n_ref):
    @pl.when(pl.program_id(1) == 0)
    def _():
        x = x_ref[...]
        n_ref[...] = _rms(x, g_ref[...]).astype(n_ref.dtype)
        o_ref[...] = x

    h = jnp.maximum(_dot(n_ref[...], w1_ref[...]), 0.0)
    o_ref[...] += _dot((h * h).astype(BF16), w2_ref[...])


def _ffn(x, g, w1, w2, *, tm, tf):
    T, D = x.shape
    F = w1.shape[1]
    return pl.pallas_call(
        _ffn_kernel,
        grid=(T // tm, F // tf),
        in_specs=[pl.BlockSpec((tm, D), lambda i, f: (i, 0)),
                  pl.BlockSpec((1, D), lambda i, f: (0, 0)),
                  pl.BlockSpec((D, tf), lambda i, f: (0, f)),
                  pl.BlockSpec((tf, D), lambda i, f: (f, 0))],
        out_specs=pl.BlockSpec((tm, D), lambda i, f: (i, 0)),
        out_shape=jax.ShapeDtypeStruct((T, D), F32),
        scratch_shapes=[pltpu.VMEM((tm, D), BF16)],
        compiler_params=_params("parallel", "arbitrary"),
        name="ffn",
    )(x, g.reshape(1, D), w1, w2)


def _rmsnorm_kernel(x_ref, g_ref, o_ref):
    o_ref[...] = _rms(x_ref[...], g_ref[...])


def _rmsnorm(x, g, row0, rows, *, tm):
    D = x.shape[1]
    assert row0 % tm == 0 and rows % tm == 0
    first = row0 // tm
    return pl.pallas_call(
        _rmsnorm_kernel,
        grid=(rows // tm,),
        in_specs=[pl.BlockSpec((tm, D), lambda i: (first + i, 0)), pl.BlockSpec((1, D), lambda i: (0, 0))],
        out_specs=pl.BlockSpec((tm, D), lambda i: (i, 0)),
        out_shape=jax.ShapeDtypeStruct((rows, D), F32),
        compiler_params=_params("parallel"),
        name="final_rmsnorm",
    )(x, g.reshape(1, D))


def _tiles(T, seqs, sgu_chunk):
    seq_gcd = math.gcd(*[length for _, length in seqs])
    return dict(
        proj=_tile(T, 1024),
        square=_tile(T, 512),
        merge=_tile(T, 1024),
        final=_tile(seq_gcd, 1024),
        conv=_tile(seq_gcd, 512, CONV_ROWS),
        sgu=_tile(T, 512, sgu_chunk),
        xattn=_tile(seq_gcd, 512),
        ffn=_tile(T, 512),
    )


def kernel(x_prompt, x_sample, mem_prompt, mem_sample, norm_mix, w_in, conv_dw, conv_b, conv_ln_g, conv_ln_b, sgu_ln_g, sgu_ln_b, sgu_w, sgu_b, na_rpb, w_br_conv, w_br_sgu, w_br_na, w_out, norm_xa, norm_mem, xa_wq, xa_wk, xa_wv, xa_wo, norm_ffn, ffn_w1, ffn_w2, norm_final):
    depth, D, d_in = w_in.shape
    Bp, Sp, _ = x_prompt.shape
    Bs, Ss, _ = x_sample.shape
    seqs = ((Bp, Sp), (Bs, Ss))
    T = Bp * Sp + Bs * Ss
    d_conv = conv_dw.shape[-1]
    d_sgu = sgu_ln_g.shape[-1]
    d_na = w_br_na.shape[1]
    na_heads = na_rpb.shape[1]
    na_cols = (na_rpb.shape[3] + 1) // 2
    d_ff = ffn_w1.shape[-1]
    n_mem = mem_prompt.shape[1]
    assert d_conv == d_sgu == d_na, "column blocks of the input projection share one width"
    width = d_conv
    gate_col0 = 7 * width
    assert d_in == gate_col0 + 3 * D and width % LANES == 0

    tiles = _tiles(T, seqs, sgu_w.shape[2])
    tn_in = _tile(d_in, 1024)
    tn_d = _tile(D, 1024)
    assert gate_col0 % tn_d == 0

    x = jnp.concatenate([x_prompt.reshape(Bp * Sp, D), x_sample.reshape(Bs * Ss, D)], axis=0)
    mem = jnp.concatenate([mem_prompt.reshape(Bp * n_mem, D), mem_sample.reshape(Bs * n_mem, D)], axis=0)
    tm_mem = _tile(mem.shape[0], 512)

    for l in range(depth):
        z = _norm_matmul(x, norm_mix[l], w_in[l].astype(BF16), tm=tiles["proj"], tn=tn_in)
        h_conv = _conv_module(z, conv_dw[l], conv_b[l], conv_ln_g[l], conv_ln_b[l], seqs, tt=tiles["conv"])
        sgu_bias = jnp.broadcast_to(sgu_b[l][:, :, None], sgu_b.shape[1:] + (d_sgu // sgu_w.shape[1],))
        h_sgu = _sgu(z, sgu_ln_g[l], sgu_ln_b[l], sgu_w[l].astype(BF16), sgu_bias, 2, 3, tt=tiles["sgu"])
        h_na = _neighbourhood_attention(z, _na_bias_table(na_rpb[l], na_cols), seqs, 4, 5, 6,
                                        width, d_na // na_heads)
        m = _merge(h_conv, h_sgu, h_na, w_br_conv[l].astype(BF16), w_br_sgu[l].astype(BF16),
                   w_br_na[l].astype(BF16), z, gate_col0, tm=tiles["merge"], tn=tn_d)
        x = _matmul_residual(m, w_out[l].astype(BF16), x, tm=tiles["square"], tn=D)
        q = _norm_matmul(x, norm_xa[l], xa_wq[l].astype(BF16), tm=tiles["square"], tn=D)
        w_kv = jnp.concatenate([xa_wk[l], xa_wv[l]], axis=1).astype(BF16)
        kv = _norm_matmul(mem, norm_mem[l], w_kv, tm=tm_mem, tn=tn_d).reshape(Bp + Bs, n_mem, 2 * D)
        o = _cross_attention(q, kv, seqs, tm=tiles["xattn"])
        x = _matmul_residual(o, xa_wo[l].astype(BF16), x, tm=tiles["square"], tn=D)
        x = _ffn(x, norm_ffn[l], ffn_w1[l].astype(BF16), ffn_w2[l].astype(BF16),
                 tm=tiles["ffn"], tf=_tile(d_ff, 1024))

    y_prompt = _rmsnorm(x, norm_final, 0, Bp * Sp, tm=tiles["final"]).reshape(Bp, Sp, D)
    y_sample = _rmsnorm(x, norm_final, Bp * Sp, Bs * Ss, tm=tiles["final"]).reshape(Bs, Ss, D)
    return (y_prompt, y_sample)
```

```python
import functools
import math

import jax
import jax.numpy as jnp
from jax import lax
from jax.experimental import pallas as pl
from jax.experimental.pallas import tpu as pltpu

F32 = jnp.float32
BF16 = jnp.bfloat16
EPS = 1e-6
MASKED = -1e30

V7X_VMEM_LIMIT_BYTES = 56 * 1024 * 1024
VMEM_PLAN_BYTES = V7X_VMEM_LIMIT_BYTES // 8 * 7
LANES = 128

GRID_W = 64
XA_HEADS = 4
CONV_HALO = 16
CONV_ROWS = 64
NA_BLOCK_ROWS = 8
NA_HALO_ROWS = 4
NA_ROWS_PER_ITER = 2


def _params(*semantics):
    return pltpu.CompilerParams(dimension_semantics=semantics,
                                vmem_limit_bytes=V7X_VMEM_LIMIT_BYTES)


def _tile(n, target, multiple=LANES):
    t = min(n, target)
    t -= t % multiple
    while t > multiple and n % t:
        t -= multiple
    assert t > 0 and n % t == 0, (n, target, multiple)
    return t


def _rms(x, g):
    ms = jnp.mean(x * x, axis=-1, keepdims=True)
    return x * lax.rsqrt(ms + EPS) * g


def _layer_norm(x, g, b):
    mu = jnp.mean(x, axis=-1, keepdims=True)
    d = x - mu
    var = jnp.mean(d * d, axis=-1, keepdims=True)
    return d * lax.rsqrt(var + EPS) * g + b


def _dot(a, b):
    return jnp.dot(a, b, preferred_element_type=F32)


def _dot_nt(a, b):
    return lax.dot_general(a, b, (((1,), (1,)), ((), ())), preferred_element_type=F32)


def _layer_weight(layer, block, index):
    return pl.BlockSpec((None,) + block, lambda *grid: (layer,) + index(*grid))


def _row_part_specs(parts, tm, cols, col_index):
    specs, first = [], 0
    for p in parts:
        n = p.shape[0] // tm
        assert n * tm == p.shape[0]
        specs.append(pl.BlockSpec(
            (tm, cols), lambda i, j, first=first, n=n: (jnp.clip(i - first, 0, n - 1), col_index(j))))
        first += n
    return specs


def _part_tiles(parts, tm):
    return tuple(p.shape[0] // tm for p in parts)


def _select_part(refs, part_tiles, i):
    x = refs[-1][...]
    end = sum(part_tiles[:-1])
    for ref, n in zip(reversed(refs[:-1]), reversed(part_tiles[:-1])):
        x = jnp.where(i < end, ref[...], x)
        end -= n
    return x


def _norm_matmul_kernel(*refs, part_tiles):
    x_refs = refs[:len(part_tiles)]
    g_ref, w_ref, o_ref, n_ref = refs[len(part_tiles):]

    @pl.when(pl.program_id(1) == 0)
    def _():
        x = _select_part(x_refs, part_tiles, pl.program_id(0))
        n_ref[...] = _rms(x, g_ref[...]).astype(n_ref.dtype)

    o_ref[...] = _dot(n_ref[...], w_ref[...]).astype(o_ref.dtype)


def _norm_matmul_vmem_bytes(n_parts, tm, D, tn):
    return n_parts * 2 * tm * D * 4 + 2 * D * tn * 2 + 2 * tm * tn * 2 + tm * D * 2 + tm * tn * 4


def _norm_matmul(x_parts, g, w, layer, *, tm, tn):
    T = sum(p.shape[0] for p in x_parts)
    D = x_parts[0].shape[1]
    N = w.shape[2]
    while _norm_matmul_vmem_bytes(len(x_parts), tm, D, tn) > VMEM_PLAN_BYTES and tm % (2 * LANES) == 0:
        tm //= 2
    return pl.pallas_call(
        functools.partial(_norm_matmul_kernel, part_tiles=_part_tiles(x_parts, tm)),
        grid=(T // tm, N // tn),
        in_specs=_row_part_specs(x_parts, tm, D, lambda j: 0) + [
            pl.BlockSpec((1, D), lambda i, j: (0, 0)),
            _layer_weight(layer, (D, tn), lambda i, j: (0, j))],
        out_specs=pl.BlockSpec((tm, tn), lambda i, j: (i, j)),
        out_shape=jax.ShapeDtypeStruct((T, N), BF16),
        scratch_shapes=[pltpu.VMEM((tm, D), BF16)],
        compiler_params=_params("parallel", "arbitrary"),
        name="norm_matmul",
    )(*x_parts, g.reshape(1, D), w)


def _matmul_residual_kernel(a_ref, w_ref, *refs, part_tiles):
    x_refs, o_ref = refs[:-1], refs[-1]
    x = _select_part(x_refs, part_tiles, pl.program_id(0))
    o_ref[...] = x + _dot(a_ref[...], w_ref[...])


def _matmul_residual(a, w, layer, x_parts, *, tm, tn):
    T, K = a.shape
    N = w.shape[2]
    return pl.pallas_call(
        functools.partial(_matmul_residual_kernel, part_tiles=_part_tiles(x_parts, tm)),
        grid=(T // tm, N // tn),
        in_specs=[pl.BlockSpec((tm, K), lambda i, j: (i, 0)),
                  _layer_weight(layer, (K, tn), lambda i, j: (0, j))]
        + _row_part_specs(x_parts, tm, tn, lambda j: j),
        out_specs=pl.BlockSpec((tm, tn), lambda i, j: (i, j)),
        out_shape=jax.ShapeDtypeStruct((T, N), F32),
        compiler_params=_params("parallel", "arbitrary"),
        name="matmul_residual",
    )(a, w, *x_parts)


def _merge_kernel(hc_ref, hs_ref, hn_ref, wc_ref, ws_ref, wn_ref, g0_ref, g1_ref, g2_ref, o_ref):
    def gate(g_ref):
        return jax.nn.sigmoid(g_ref[...].astype(F32))

    m = gate(g0_ref) * _dot(hc_ref[...], wc_ref[...])
    m += gate(g1_ref) * _dot(hs_ref[...], ws_ref[...])
    m += gate(g2_ref) * _dot(hn_ref[...], wn_ref[...])
    o_ref[...] = m.astype(o_ref.dtype)


def _merge(hc, hs, hn, wc, ws, wn, layer, z, gate_col0, *, tm, tn):
    T = hc.shape[0]
    D = wc.shape[2]
    gate_block0 = gate_col0 // tn
    blocks_per_gate = D // tn

    def h_spec(h):
        return pl.BlockSpec((tm, h.shape[1]), lambda i, j: (i, 0))

    def w_spec(w):
        return _layer_weight(layer, (w.shape[1], tn), lambda i, j: (0, j))

    def g_spec(b):
        return pl.BlockSpec((tm, tn), lambda i, j: (i, gate_block0 + b * blocks_per_gate + j))

    return pl.pallas_call(
        _merge_kernel,
        grid=(T // tm, D // tn),
        in_specs=[h_spec(hc), h_spec(hs), h_spec(hn), w_spec(wc), w_spec(ws), w_spec(wn),
                  g_spec(0), g_spec(1), g_spec(2)],
        out_specs=pl.BlockSpec((tm, tn), lambda i, j: (i, j)),
        out_shape=jax.ShapeDtypeStruct((T, D), BF16),
        compiler_params=_params("parallel", "arbitrary"),
        name="branch_merge",
    )(hc, hs, hn, wc, ws, wn, z, z, z)


def _seq_position(t0, seqs):
    (n_a, len_a), (_, len_b) = seqs
    split = n_a * len_a
    in_a = t0 < split
    pos = jnp.where(in_a, lax.rem(t0, len_a), lax.rem(t0 - split, len_b))
    return pos, jnp.where(in_a, len_a, len_b)


def _conv_kernel(a_ref, b_ref, ap_ref, bp_ref, an_ref, bn_ref, w_ref, cb_ref, g_ref, beta_ref,
                 o_ref, hs_ref, y_ref, *, tt, seqs):
    width = w_ref.shape[0]
    C = a_ref.shape[1]
    first_tap = CONV_HALO - width // 2

    def glu(a, b):
        return a[...].astype(F32) * jax.nn.sigmoid(b[...].astype(F32))

    pos, seq_len = _seq_position(pl.program_id(0) * tt, seqs)
    hs_ref[0:CONV_HALO, :] = jnp.where(pos == 0, 0.0, glu(ap_ref, bp_ref))
    hs_ref[CONV_HALO:CONV_HALO + tt, :] = glu(a_ref, b_ref)
    hs_ref[CONV_HALO + tt:, :] = jnp.where(pos + tt == seq_len, 0.0, glu(an_ref, bn_ref))

    span = CONV_ROWS + 2 * CONV_HALO

    def rows(r, carry):
        r0 = pl.multiple_of(r * CONV_ROWS, CONV_ROWS)
        for c in range(C // LANES):
            lanes = slice(c * LANES, (c + 1) * LANES)
            win = hs_ref[pl.ds(r0, span), lanes]
            acc = jnp.zeros((CONV_ROWS, LANES), F32)
            for phase in range(8):
                taps = [j for j in range(width) if (first_tap + j) % 8 == phase]
                if not taps:
                    continue
                shifted = win if phase == 0 else pltpu.roll(win, span - phase, axis=0)
                for j in taps:
                    a8 = first_tap + j - phase
                    acc = acc + w_ref[j:j + 1, lanes] * shifted[a8:a8 + CONV_ROWS]
            y_ref[pl.ds(r0, CONV_ROWS), lanes] = acc + cb_ref[:, lanes]
        return carry

    lax.fori_loop(0, tt // CONV_ROWS, rows, 0)

    y = _layer_norm(y_ref[...], g_ref[...], beta_ref[...])
    o_ref[...] = (y * jax.nn.sigmoid(y)).astype(o_ref.dtype)


def _conv_module(z, conv_dw, conv_b, ln_g, ln_b, seqs, *, tt):
    T = z.shape[0]
    width, C = conv_dw.shape
    assert width // 2 <= CONV_HALO and tt % CONV_ROWS == 0 and tt % CONV_HALO == 0
    assert 7 + ((CONV_HALO - width // 2 + width - 1) // 8) * 8 + CONV_ROWS <= CONV_ROWS + 2 * CONV_HALO
    halo_per_tile = tt // CONV_HALO
    last_halo = T // CONV_HALO - 1

    def cur(col):
        return pl.BlockSpec((tt, C), lambda i: (i, col))

    def prev(col):
        return pl.BlockSpec((CONV_HALO, C), lambda i: (jnp.maximum(i * halo_per_tile - 1, 0), col))

    def nxt(col):
        return pl.BlockSpec((CONV_HALO, C),
                            lambda i: (jnp.minimum((i + 1) * halo_per_tile, last_halo), col))

    def vec():
        return pl.BlockSpec((1, C), lambda i: (0, 0))

    return pl.pallas_call(
        functools.partial(_conv_kernel, tt=tt, seqs=seqs),
        grid=(T // tt,),
        in_specs=[cur(0), cur(1), prev(0), prev(1), nxt(0), nxt(1),
                  pl.BlockSpec((width, C), lambda i: (0, 0)), vec(), vec(), vec()],
        out_specs=pl.BlockSpec((tt, C), lambda i: (i, 0)),
        out_shape=jax.ShapeDtypeStruct((T, C), BF16),
        scratch_shapes=[pltpu.VMEM((tt + 2 * CONV_HALO, C), F32), pltpu.VMEM((tt, C), F32)],
        compiler_params=_params("parallel"),
        name="conv_module",
    )(z, z, z, z, z, z, conv_dw, conv_b.reshape(1, C), ln_g.reshape(1, C), ln_b.reshape(1, C))


def _sgu_kernel(u_ref, v_ref, g_ref, beta_ref, w_ref, bias_ref, o_ref, vn_ref):
    groups, chunk, _ = w_ref.shape
    tt, C = v_ref.shape
    gd = C // groups
    vn_ref[...] = _layer_norm(v_ref[...].astype(F32), g_ref[...], beta_ref[...]).astype(vn_ref.dtype)
    for n in range(tt // chunk):
        rows = slice(n * chunk, (n + 1) * chunk)
        for g in range(groups):
            lanes = slice(g * gd, (g + 1) * gd)
            s = _dot(w_ref[g], vn_ref[rows, lanes]) + bias_ref[g]
            o_ref[rows, lanes] = (u_ref[rows, lanes].astype(F32) * s).astype(o_ref.dtype)


def _sgu(z, ln_g, ln_b, sgu_w, sgu_bias, u_col, v_col, *, tt):
    T = z.shape[0]
    groups, chunk, _ = sgu_w.shape
    C = ln_g.shape[0]
    gd = C // groups
    assert gd % LANES == 0 and tt % chunk == 0

    def vec():
        return pl.BlockSpec((1, C), lambda i: (0, 0))

    return pl.pallas_call(
        _sgu_kernel,
        grid=(T // tt,),
        in_specs=[pl.BlockSpec((tt, C), lambda i: (i, u_col)),
                  pl.BlockSpec((tt, C), lambda i: (i, v_col)),
                  vec(), vec(),
                  pl.BlockSpec((groups, chunk, chunk), lambda i: (0, 0, 0)),
                  pl.BlockSpec((groups, chunk, gd), lambda i: (0, 0, 0))],
        out_specs=pl.BlockSpec((tt, C), lambda i: (i, 0)),
        out_shape=jax.ShapeDtypeStruct((T, C), BF16),
        scratch_shapes=[pltpu.VMEM((tt, C), BF16)],
        compiler_params=_params("parallel"),
        name="sgu",
    )(z, z, ln_g.reshape(1, C), ln_b.reshape(1, C), sgu_w, sgu_bias)


def _na_bias_table(rpb, na_cols):
    H, n_dr, _ = rpb.shape
    c = jnp.arange(GRID_W)
    c_start = jnp.clip(c - na_cols // 2, 0, GRID_W - na_cols)
    kc = jnp.arange(GRID_W)
    valid = (kc[None, :] >= c_start[:, None]) & (kc[None, :] < c_start[:, None] + na_cols)
    dc = jnp.clip(kc[None, :] - c[:, None] + (na_cols - 1), 0, 2 * na_cols - 2)
    dense = jnp.where(valid[None, None], rpb[:, :, dc].astype(F32), MASKED)
    pairs = jnp.concatenate([dense[:, :-1], dense[:, 1:]], axis=-1)
    pairs = pairs.reshape(H // 2, 2, n_dr - 1, GRID_W, 2 * GRID_W).transpose(0, 2, 1, 3, 4)
    return pairs.reshape(H // 2, n_dr - 1, 2 * GRID_W, 2 * GRID_W)


def _na_kernel(q_ref, kp_ref, kc_ref, kn_ref, vp_ref, vc_ref, vn_ref, tab_ref, o_ref,
               kbuf, vbuf, *, seqs, kr, scale):
    halo = NA_HALO_ROWS * GRID_W
    blk = NA_BLOCK_ROWS * GRID_W
    for buf, p, c, n in ((kbuf, kp_ref, kc_ref, kn_ref), (vbuf, vp_ref, vc_ref, vn_ref)):
        buf[0:halo, :] = p[...]
        buf[halo:halo + blk, :] = c[...]
        buf[halo + blk:, :] = n[...]

    n_pairs = q_ref.shape[1] // LANES
    row_seqs = tuple((n, length // GRID_W) for n, length in seqs)
    loc0, seq_rows = _seq_position(pl.program_id(0) * NA_BLOCK_ROWS, row_seqs)
    lane = lax.broadcasted_iota(jnp.int32, (GRID_W, LANES), 1)
    low = lane < GRID_W

    pair_lanes = [slice(hp * LANES, (hp + 1) * LANES) for hp in range(n_pairs)]

    def geometry(j):
        r_loc = loc0 + j
        r_start = jnp.clip(r_loc - kr // 2, 0, seq_rows - kr)
        dr0 = r_start - r_loc + (kr - 1)
        koff = pl.multiple_of((r_start - loc0 + NA_HALO_ROWS) * GRID_W, GRID_W)
        qoff = pl.multiple_of(j * GRID_W, GRID_W)
        return qoff, koff, dr0

    def score_pass(qoff, koff):
        scores = []
        for lanes in pair_lanes:
            q2 = q_ref[pl.ds(qoff, GRID_W), lanes].astype(F32) * scale
            qq = jnp.concatenate([jnp.where(low, q2, 0.0), jnp.where(low, 0.0, q2)], axis=0)
            scores.append(_dot_nt(qq.astype(BF16), kbuf[pl.ds(koff, kr * GRID_W), lanes]))
        return scores

    def softmax_pass(scores, dr0):
        probs, inv_sums = [], []
        for hp, s in enumerate(scores):
            s = s + jnp.concatenate([tab_ref[hp, dr0 + 2 * t] for t in range(kr // 2)], axis=1)
            p = jnp.exp(s - jnp.max(s, axis=-1, keepdims=True))
            inv_sums.append(1.0 / jnp.sum(p, axis=-1, keepdims=True))
            probs.append(p.astype(BF16))
        return probs, inv_sums

    def value_pass(probs, inv_sums, qoff, koff):
        for lanes, p, inv in zip(pair_lanes, probs, inv_sums):
            r = _dot(p, vbuf[pl.ds(koff, kr * GRID_W), lanes]) * inv
            o_ref[pl.ds(qoff, GRID_W), lanes] = jnp.where(low, r[:GRID_W], r[GRID_W:]).astype(o_ref.dtype)

    def rows(jj, carry):
        geo = [geometry(jj * NA_ROWS_PER_ITER + r) for r in range(NA_ROWS_PER_ITER)]
        scores = [score_pass(qoff, koff) for qoff, koff, _ in geo]
        for (qoff, koff, dr0), sc in zip(geo, scores):
            probs, inv_sums = softmax_pass(sc, dr0)
            value_pass(probs, inv_sums, qoff, koff)
        return carry

    lax.fori_loop(0, NA_BLOCK_ROWS // NA_ROWS_PER_ITER, rows, 0)


def _neighbourhood_attention(z, table, seqs, q_col, k_col, v_col, width, head_dim):
    T = z.shape[0]
    kr = table.shape[1] // 2 + 1
    assert head_dim * 2 == LANES and GRID_W * 2 == LANES and kr % 2 == 0
    assert kr == NA_BLOCK_ROWS and NA_HALO_ROWS >= kr // 2
    for _, length in seqs:
        assert length % (NA_BLOCK_ROWS * GRID_W) == 0 and length // GRID_W >= kr
    blk = NA_BLOCK_ROWS * GRID_W
    halo = NA_HALO_ROWS * GRID_W
    halo_per_blk = blk // halo
    last_halo = T // halo - 1

    def cur(col):
        return pl.BlockSpec((blk, width), lambda i: (i, col))

    def prev(col):
        return pl.BlockSpec((halo, width), lambda i: (jnp.maximum(i * halo_per_blk - 1, 0), col))

    def nxt(col):
        return pl.BlockSpec((halo, width),
                            lambda i: (jnp.minimum((i + 1) * halo_per_blk, last_halo), col))

    return pl.pallas_call(
        functools.partial(_na_kernel, seqs=seqs, kr=kr, scale=head_dim ** -0.5),
        grid=(T // blk,),
        in_specs=[cur(q_col), prev(k_col), cur(k_col), nxt(k_col),
                  prev(v_col), cur(v_col), nxt(v_col),
                  pl.BlockSpec(table.shape, lambda i: (0, 0, 0, 0))],
        out_specs=pl.BlockSpec((blk, width), lambda i: (i, 0)),
        out_shape=jax.ShapeDtypeStruct((T, width), BF16),
        scratch_shapes=[pltpu.VMEM((blk + 2 * halo, width), BF16),
                        pltpu.VMEM((blk + 2 * halo, width), BF16)],
        compiler_params=_params("parallel"),
        name="neighbourhood_attention",
    )(z, z, z, z, z, z, z, table)


def _xattn_kernel(q_ref, k_ref, v_ref, o_ref, *, scale):
    hd = q_ref.shape[1] // XA_HEADS
    for h in range(XA_HEADS):
        lanes = slice(h * hd, (h + 1) * hd)
        s = _dot_nt(q_ref[:, lanes], k_ref[0, :, lanes]) * scale
        m = jnp.max(s, axis=-1, keepdims=True)
        p = jnp.exp(s - m)
        l = jnp.sum(p, axis=-1, keepdims=True)
        o_ref[:, lanes] = (_dot(p.astype(BF16), v_ref[0, :, lanes]) / l).astype(o_ref.dtype)


def _cross_attention(q, kv, seqs, *, tm):
    T, D = q.shape
    n_mem = kv.shape[1]
    (n_a, len_a), (_, len_b) = seqs
    assert len_a % tm == 0 and len_b % tm == 0 and (D // XA_HEADS) % LANES == 0
    split = n_a * len_a

    def seq_of(i):
        t0 = i * tm
        return jnp.where(t0 < split, t0 // len_a, n_a + (t0 - split) // len_b)

    return pl.pallas_call(
        functools.partial(_xattn_kernel, scale=(D // XA_HEADS) ** -0.5),
        grid=(T // tm,),
        in_specs=[pl.BlockSpec((tm, D), lambda i: (i, 0)),
                  pl.BlockSpec((1, n_mem, D), lambda i: (seq_of(i), 0, 0)),
                  pl.BlockSpec((1, n_mem, D), lambda i: (seq_of(i), 0, 1))],
        out_specs=pl.BlockSpec((tm, D), lambda i: (i, 0)),
        out_shape=jax.ShapeDtypeStruct((T, D), BF16),
        compiler_params=_params("parallel"),
        name="cross_attention",
    )(q, kv, kv)


def _ffn_kernel(x_ref, g_ref, w1_ref, w2_ref, o_ref, n_ref):
    @pl.when(pl.program_id(1) == 0)
    def _():
        x = x_ref[...]
        n_ref[...] = _rms(x, g_ref[...]).astype(n_ref.dtype)
        o_ref[...] = x

    h = jnp.maximum(_dot(n_ref[...], w1_ref[...]), 0.0)
    o_ref[...] += _dot((h * h).astype(BF16), w2_ref[...])


def _ffn(x, g, w1, w2, layer, *, tm, tf):
    T, D = x.shape
    F = w1.shape[2]
    return pl.pallas_call(
        _ffn_kernel,
        grid=(T // tm, F // tf),
        in_specs=[pl.BlockSpec((tm, D), lambda i, f: (i, 0)),
                  pl.BlockSpec((1, D), lambda i, f: (0, 0)),
                  _layer_weight(layer, (D, tf), lambda i, f: (0, f)),
                  _layer_weight(layer, (tf, D), lambda i, f: (f, 0))],
        out_specs=pl.BlockSpec((tm, D), lambda i, f: (i, 0)),
        out_shape=jax.ShapeDtypeStruct((T, D), F32),
        scratch_shapes=[pltpu.VMEM((tm, D), BF16)],
        compiler_params=_params("parallel", "arbitrary"),
        name="ffn",
    )(x, g.reshape(1, D), w1, w2)


def _rmsnorm_kernel(x_ref, g_ref, o_ref):
    o_ref[...] = _rms(x_ref[...], g_ref[...])


def _rmsnorm(x, g, row0, rows, *, tm):
    D = x.shape[1]
    assert row0 % tm == 0 and rows % tm == 0
    first = row0 // tm
    return pl.pallas_call(
        _rmsnorm_kernel,
        grid=(rows // tm,),
        in_specs=[pl.BlockSpec((tm, D), lambda i: (first + i, 0)), pl.BlockSpec((1, D), lambda i: (0, 0))],
        out_specs=pl.BlockSpec((tm, D), lambda i: (i, 0)),
        out_shape=jax.ShapeDtypeStruct((rows, D), F32),
        compiler_params=_params("parallel"),
        name="final_rmsnorm",
    )(x, g.reshape(1, D))


def _tiles(T, seqs, sgu_chunk):
    seq_gcd = math.gcd(*[length for _, length in seqs])
    group_gcd = math.gcd(*[n * length for n, length in seqs])
    return dict(
        proj=_tile(group_gcd, 1024),
        square=_tile(group_gcd, 512),
        merge=_tile(T, 1024),
        final=_tile(seq_gcd, 1024),
        conv=_tile(seq_gcd, 512, CONV_ROWS),
        sgu=_tile(T, 512, sgu_chunk),
        xattn=_tile(seq_gcd, 512),
        ffn=_tile(T, 512),
    )


def kernel(x_prompt, x_sample, mem_prompt, mem_sample, norm_mix, w_in, conv_dw, conv_b, conv_ln_g, conv_ln_b, sgu_ln_g, sgu_ln_b, sgu_w, sgu_b, na_rpb, w_br_conv, w_br_sgu, w_br_na, w_out, norm_xa, norm_mem, xa_wq, xa_wk, xa_wv, xa_wo, norm_ffn, ffn_w1, ffn_w2, norm_final):
    depth, D, d_in = w_in.shape
    Bp, Sp, _ = x_prompt.shape
    Bs, Ss, _ = x_sample.shape
    seqs = ((Bp, Sp), (Bs, Ss))
    T = Bp * Sp + Bs * Ss
    d_conv = conv_dw.shape[-1]
    d_sgu = sgu_ln_g.shape[-1]
    d_na = w_br_na.shape[1]
    na_heads = na_rpb.shape[1]
    na_cols = (na_rpb.shape[3] + 1) // 2
    d_ff = ffn_w1.shape[-1]
    n_mem = mem_prompt.shape[1]
    assert d_conv == d_sgu == d_na, "column blocks of the input projection share one width"
    width = d_conv
    gate_col0 = 7 * width
    assert d_in == gate_col0 + 3 * D and width % LANES == 0

    tiles = _tiles(T, seqs, sgu_w.shape[2])
    tn_in = _tile(d_in, 1024)
    tn_d = _tile(D, 1024)
    assert gate_col0 % tn_d == 0

    x_parts = (x_prompt.reshape(Bp * Sp, D), x_sample.reshape(Bs * Ss, D))
    mem =jnp.concatenate([mem_prompt.reshape(Bp * n_mem, D), mem_sample.reshape(Bs * n_mem, D)], axis=0)
    tm_mem = _tile(mem.shape[0], 512)

    w_in, w_br_conv, w_br_sgu, w_br_na, w_out, xa_wq, xa_wo, ffn_w1, ffn_w2 = (
        w.astype(BF16) for w in (w_in, w_br_conv, w_br_sgu, w_br_na, w_out, xa_wq, xa_wo, ffn_w1, ffn_w2))
    w_kv = jnp.concatenate([xa_wk, xa_wv], axis=2).astype(BF16)

    for l in range(depth):
        z = _norm_matmul(x_parts, norm_mix[l], w_in, l, tm=tiles["proj"], tn=tn_in)
        h_conv = _conv_module(z, conv_dw[l], conv_b[l], conv_ln_g[l], conv_ln_b[l], seqs, tt=tiles["conv"])
        sgu_bias = jnp.broadcast_to(sgu_b[l][:, :, None], sgu_b.shape[1:] + (d_sgu // sgu_w.shape[1],))
        h_sgu = _sgu(z, sgu_ln_g[l], sgu_ln_b[l], sgu_w[l].astype(BF16), sgu_bias, 2, 3, tt=tiles["sgu"])
        h_na = _neighbourhood_attention(z, _na_bias_table(na_rpb[l], na_cols), seqs, 4, 5, 6,
                                        width, d_na // na_heads)
        m = _merge(h_conv, h_sgu, h_na, w_br_conv, w_br_sgu, w_br_na, l, z, gate_col0,
                   tm=tiles["merge"], tn=tn_d)
        x = _matmul_residual(m, w_out, l, x_parts, tm=tiles["square"], tn=D)
        q = _norm_matmul((x,), norm_xa[l], xa_wq, l, tm=tiles["square"], tn=D)
        kv = _norm_matmul((mem,), norm_mem[l], w_kv, l, tm=tm_mem, tn=tn_d).reshape(Bp + Bs, n_mem, 2 * D)
        o = _cross_attention(q, kv, seqs, tm=tiles["xattn"])
        x = _matmul_residual(o, xa_wo, l, (x,), tm=tiles["square"], tn=D)
        x = _ffn(x, norm_ffn[l], ffn_w1, ffn_w2, l, tm=tiles["ffn"], tf=_tile(d_ff, 1024))
        x_parts = (x,)

    y_prompt = _rmsnorm(x, norm_final, 0, Bp * Sp, tm=tiles["final"]).reshape(Bp, Sp, D)
    y_sample = _rmsnorm(x, norm_final, Bp * Sp, Bs * Ss, tm=tiles["final"]).reshape(Bs, Ss, D)
    return (y_prompt, y_sample)
```

```python
import functools
import math

import jax
import jax.numpy as jnp
from jax import lax
from jax.experimental import pallas as pl
from jax.experimental.pallas import tpu as pltpu

F32 = jnp.float32
BF16 = jnp.bfloat16
EPS = 1e-6
MASKED = -1e30

V7X_VMEM_LIMIT_BYTES = 60000 * 1024
VMEM_PLAN_BYTES = V7X_VMEM_LIMIT_BYTES - 4 * 1024 * 1024
LANES = 128
V7X_MXU_COLUMNS = 256

GRID_W = 64
XA_HEADS = 4
CONV_HALO = 16
CONV_ROWS = 64
NA_BLOCK_ROWS = 8
NA_HALO_ROWS = 4
NA_ROWS_PER_ITER = 2


def _params(*semantics):
    return pltpu.CompilerParams(dimension_semantics=semantics,
                                vmem_limit_bytes=V7X_VMEM_LIMIT_BYTES)


def _tile(n, target, multiple=LANES):
    t = min(n, target)
    t -= t % multiple
    while t > multiple and n % t:
        t -= multiple
    assert t > 0 and n % t == 0, (n, target, multiple)
    return t


def _rms(x, g):
    ms = jnp.mean(x * x, axis=-1, keepdims=True)
    return x * lax.rsqrt(ms + EPS) * g


def _layer_norm(x, g, b):
    mu = jnp.mean(x, axis=-1, keepdims=True)
    d = x - mu
    var = jnp.mean(d * d, axis=-1, keepdims=True)
    return d * lax.rsqrt(var + EPS) * g + b


def _dot(a, b):
    return jnp.dot(a, b, preferred_element_type=F32)


def _dot_nt(a, b):
    return lax.dot_general(a, b, (((1,), (1,)), ((), ())), preferred_element_type=F32)


def _layer_weight(layer, block, index):
    return pl.BlockSpec((None,) + block, lambda *grid: (layer,) + index(*grid))


def _row_part_specs(parts, tm, cols, col_index):
    specs, first = [], 0
    for p in parts:
        n = p.shape[0] // tm
        assert n * tm == p.shape[0]
        specs.append(pl.BlockSpec(
            (tm, cols), lambda i, j, first=first, n=n: (jnp.clip(i - first, 0, n - 1), col_index(j))))
        first += n
    return specs


def _part_tiles(parts, tm):
    return tuple(p.shape[0] // tm for p in parts)


def _with_part(refs, part_tiles, i, body):
    if len(refs) == 1:
        body(refs[0])
        return
    start = 0
    for ref, n in zip(refs, part_tiles):
        pl.when((i >= start) & (i < start + n))(functools.partial(body, ref))
        start += n


def _norm_matmul_kernel(*refs, part_tiles):
    x_refs = refs[:len(part_tiles)]
    g_ref, w_ref, o_ref, n_ref = refs[len(part_tiles):]

    def normalise(x_ref):
        n_ref[...] = _rms(x_ref[...], g_ref[...]).astype(n_ref.dtype)

    @pl.when(pl.program_id(1) == 0)
    def _():
        _with_part(x_refs, part_tiles, pl.program_id(0), normalise)

    o_ref[...] = _dot(n_ref[...], w_ref[...]).astype(o_ref.dtype)


def _norm_matmul_vmem_bytes(n_parts, tm, D, tn):
    return n_parts * 2 * tm * D * 4 + 2 * D * tn * 2 + 2 * tm * tn * 2 + tm * D * 2 + tm * tn * 4


def _norm_matmul_tiles(n_parts, T, rows_gcd, D, N):
    best = None
    for tm in (1024, 512, 256):
        if rows_gcd % tm:
            continue
        for tn in range(V7X_MXU_COLUMNS, N + 1, V7X_MXU_COLUMNS):
            if N % tn == 0 and _norm_matmul_vmem_bytes(n_parts, tm, D, tn) <= VMEM_PLAN_BYTES:
                steps = (T // tm) * (N // tn)
                if best is None or steps < best[0]:
                    best = (steps, tm, tn)
    assert best is not None, (n_parts, T, rows_gcd, D, N)
    return best[1:]


def _norm_matmul(x_parts, g, w, layer):
    T = sum(p.shape[0] for p in x_parts)
    D = x_parts[0].shape[1]
    N = w.shape[2]
    tm, tn = _norm_matmul_tiles(len(x_parts), T, math.gcd(*[p.shape[0] for p in x_parts]), D, N)
    return pl.pallas_call(
        functools.partial(_norm_matmul_kernel, part_tiles=_part_tiles(x_parts, tm)),
        grid=(T // tm, N // tn),
        in_specs=_row_part_specs(x_parts, tm, D, lambda j: 0) + [
            pl.BlockSpec((1, D), lambda i, j: (0, 0)),
            _layer_weight(layer, (D, tn), lambda i, j: (0, j))],
        out_specs=pl.BlockSpec((tm, tn), lambda i, j: (i, j)),
        out_shape=jax.ShapeDtypeStruct((T, N), BF16),
        scratch_shapes=[pltpu.VMEM((tm, D), BF16)],
        compiler_params=_params("parallel", "arbitrary"),
        name="norm_matmul",
    )(*x_parts, g.reshape(1, D), w)


def _matmul_residual_kernel(a_ref, w_ref, *refs, part_tiles):
    x_refs, o_ref = refs[:-1], refs[-1]
    update = _dot(a_ref[...], w_ref[...])

    def add_residual(x_ref):
        o_ref[...] = x_ref[...] + update

    _with_part(x_refs, part_tiles, pl.program_id(0), add_residual)


def _matmul_residual(a, w, layer, x_parts, *, tm, tn):
    T, K = a.shape
    N = w.shape[2]
    return pl.pallas_call(
        functools.partial(_matmul_residual_kernel, part_tiles=_part_tiles(x_parts, tm)),
        grid=(T // tm, N // tn),
        in_specs=[pl.BlockSpec((tm, K), lambda i, j: (i, 0)),
                  _layer_weight(layer, (K, tn), lambda i, j: (0, j))]
        + _row_part_specs(x_parts, tm, tn, lambda j: j),
        out_specs=pl.BlockSpec((tm, tn), lambda i, j: (i, j)),
        out_shape=jax.ShapeDtypeStruct((T, N), F32),
        compiler_params=_params("parallel", "arbitrary"),
        name="matmul_residual",
    )(a, w, *x_parts)


def _merge_kernel(hc_ref, hs_ref, hn_ref, wc_ref, ws_ref, wn_ref, g0_ref, g1_ref, g2_ref, o_ref):
    def gate(g_ref):
        return jax.nn.sigmoid(g_ref[...].astype(F32))

    m = gate(g0_ref) * _dot(hc_ref[...], wc_ref[...])
    m += gate(g1_ref) * _dot(hs_ref[...], ws_ref[...])
    m += gate(g2_ref) * _dot(hn_ref[...], wn_ref[...])
    o_ref[...] = m.astype(o_ref.dtype)


def _merge(hc, hs, hn, wc, ws, wn, layer, z, gate_col0, *, tm, tn):
    T = hc.shape[0]
    D = wc.shape[2]
    gate_block0 = gate_col0 // tn
    blocks_per_gate = D // tn

    def h_spec(h):
        return pl.BlockSpec((tm, h.shape[1]), lambda i, j: (i, 0))

    def w_spec(w):
        return _layer_weight(layer, (w.shape[1], tn), lambda i, j: (0, j))

    def g_spec(b):
        return pl.BlockSpec((tm, tn), lambda i, j: (i, gate_block0 + b * blocks_per_gate + j))

    return pl.pallas_call(
        _merge_kernel,
        grid=(T // tm, D // tn),
        in_specs=[h_spec(hc), h_spec(hs), h_spec(hn), w_spec(wc), w_spec(ws), w_spec(wn),
                  g_spec(0), g_spec(1), g_spec(2)],
        out_specs=pl.BlockSpec((tm, tn), lambda i, j: (i, j)),
        out_shape=jax.ShapeDtypeStruct((T, D), BF16),
        compiler_params=_params("parallel", "arbitrary"),
        name="branch_merge",
    )(hc, hs, hn, wc, ws, wn, z, z, z)


def _seq_position(t0, seqs):
    (n_a, len_a), (_, len_b) = seqs
    split = n_a * len_a
    in_a = t0 < split
    pos = jnp.where(in_a, lax.rem(t0, len_a), lax.rem(t0 - split, len_b))
    return pos, jnp.where(in_a, len_a, len_b)


def _conv_kernel(a_ref, b_ref, ap_ref, bp_ref, an_ref, bn_ref, w_ref, cb_ref, g_ref, beta_ref,
                 o_ref, hs_ref, y_ref, *, tt, seqs):
    width = w_ref.shape[0]
    C = a_ref.shape[1]
    first_tap = CONV_HALO - width // 2

    def glu(a, b):
        return a[...].astype(F32) * jax.nn.sigmoid(b[...].astype(F32))

    pos, seq_len = _seq_position(pl.program_id(0) * tt, seqs)
    hs_ref[0:CONV_HALO, :] = jnp.where(pos == 0, 0.0, glu(ap_ref, bp_ref))
    hs_ref[CONV_HALO:CONV_HALO + tt, :] = glu(a_ref, b_ref)
    hs_ref[CONV_HALO + tt:, :] = jnp.where(pos + tt == seq_len, 0.0, glu(an_ref, bn_ref))

    span = CONV_ROWS + 2 * CONV_HALO

    def rows(r, carry):
        r0 = pl.multiple_of(r * CONV_ROWS, CONV_ROWS)
        for c in range(C // LANES):
            lanes = slice(c * LANES, (c + 1) * LANES)
            win = hs_ref[pl.ds(r0, span), lanes]
            acc = jnp.zeros((CONV_ROWS, LANES), F32)
            for phase in range(8):
                taps = [j for j in range(width) if (first_tap + j) % 8 == phase]
                if not taps:
                    continue
                shifted = win if phase == 0 else pltpu.roll(win, span - phase, axis=0)
                for j in taps:
                    a8 = first_tap + j - phase
                    acc = acc + w_ref[j:j + 1, lanes] * shifted[a8:a8 + CONV_ROWS]
            y_ref[pl.ds(r0, CONV_ROWS), lanes] = acc + cb_ref[:, lanes]
        return carry

    lax.fori_loop(0, tt // CONV_ROWS, rows, 0)

    y = _layer_norm(y_ref[...], g_ref[...], beta_ref[...])
    o_ref[...] = (y * jax.nn.sigmoid(y)).astype(o_ref.dtype)


def _conv_module(z, conv_dw, conv_b, ln_g, ln_b, seqs, *, tt):
    T = z.shape[0]
    width, C = conv_dw.shape
    assert width // 2 <= CONV_HALO and tt % CONV_ROWS == 0 and tt % CONV_HALO == 0
    assert 7 + ((CONV_HALO - width // 2 + width - 1) // 8) * 8 + CONV_ROWS <= CONV_ROWS + 2 * CONV_HALO
    halo_per_tile = tt // CONV_HALO
    last_halo = T // CONV_HALO - 1

    def cur(col):
        return pl.BlockSpec((tt, C), lambda i: (i, col))

    def prev(col):
        return pl.BlockSpec((CONV_HALO, C), lambda i: (jnp.maximum(i * halo_per_tile - 1, 0), col))

    def nxt(col):
        return pl.BlockSpec((CONV_HALO, C),
                            lambda i: (jnp.minimum((i + 1) * halo_per_tile, last_halo), col))

    def vec():
        return pl.BlockSpec((1, C), lambda i: (0, 0))

    return pl.pallas_call(
        functools.partial(_conv_kernel, tt=tt, seqs=seqs),
        grid=(T // tt,),
        in_specs=[cur(0), cur(1), prev(0), prev(1), nxt(0), nxt(1),
                  pl.BlockSpec((width, C), lambda i: (0, 0)), vec(), vec(), vec()],
        out_specs=pl.BlockSpec((tt, C), lambda i: (i, 0)),
        out_shape=jax.ShapeDtypeStruct((T, C), BF16),
        scratch_shapes=[pltpu.VMEM((tt + 2 * CONV_HALO, C), F32), pltpu.VMEM((tt, C), F32)],
        compiler_params=_params("parallel"),
        name="conv_module",
    )(z, z, z, z, z, z, conv_dw, conv_b.reshape(1, C), ln_g.reshape(1, C), ln_b.reshape(1, C))


def _sgu_kernel(u_ref, v_ref, g_ref, beta_ref, w_ref, bias_ref, o_ref, vn_ref):
    groups, chunk, _ = w_ref.shape
    tt, C = v_ref.shape
    gd = C // groups
    vn_ref[...] = _layer_norm(v_ref[...].astype(F32), g_ref[...], beta_ref[...]).astype(vn_ref.dtype)
    for n in range(tt // chunk):
        rows = slice(n * chunk, (n + 1) * chunk)
        for g in range(groups):
            lanes = slice(g * gd, (g + 1) * gd)
            s = _dot(w_ref[g], vn_ref[rows, lanes]) + bias_ref[g]
            o_ref[rows, lanes] = (u_ref[rows, lanes].astype(F32) * s).astype(o_ref.dtype)


def _sgu(z, ln_g, ln_b, sgu_w, sgu_bias, u_col, v_col, *, tt):
    T = z.shape[0]
    groups, chunk, _ = sgu_w.shape
    C = ln_g.shape[0]
    gd = C // groups
    assert gd % LANES == 0 and tt % chunk == 0

    def vec():
        return pl.BlockSpec((1, C), lambda i: (0, 0))

    return pl.pallas_call(
        _sgu_kernel,
        grid=(T // tt,),
        in_specs=[pl.BlockSpec((tt, C), lambda i: (i, u_col)),
                  pl.BlockSpec((tt, C), lambda i: (i, v_col)),
                  vec(), vec(),
                  pl.BlockSpec((groups, chunk, chunk), lambda i: (0, 0, 0)),
                  pl.BlockSpec((groups, chunk, gd), lambda i: (0, 0, 0))],
        out_specs=pl.BlockSpec((tt, C), lambda i: (i, 0)),
        out_shape=jax.ShapeDtypeStruct((T, C), BF16),
        scratch_shapes=[pltpu.VMEM((tt, C), BF16)],
        compiler_params=_params("parallel"),
        name="sgu",
    )(z, z, ln_g.reshape(1, C), ln_b.reshape(1, C), sgu_w, sgu_bias)


def _na_bias_table(rpb, na_cols):
    H, n_dr, _ = rpb.shape
    c = jnp.arange(GRID_W)
    c_start = jnp.clip(c - na_cols // 2, 0, GRID_W - na_cols)
    kc = jnp.arange(GRID_W)
    valid = (kc[None, :] >= c_start[:, None]) & (kc[None, :] < c_start[:, None] + na_cols)
    dc = jnp.clip(kc[None, :] - c[:, None] + (na_cols - 1), 0, 2 * na_cols - 2)
    dense = jnp.where(valid[None, None], rpb[:, :, dc].astype(F32), MASKED)
    pairs = jnp.concatenate([dense[:, :-1], dense[:, 1:]], axis=-1)
    pairs = pairs.reshape(H // 2, 2, n_dr - 1, GRID_W, 2 * GRID_W).transpose(0, 2, 1, 3, 4)
    return pairs.reshape(H // 2, n_dr - 1, 2 * GRID_W, 2 * GRID_W)


def _na_kernel(q_ref, kp_ref, kc_ref, kn_ref, vp_ref, vc_ref, vn_ref, tab_ref, o_ref,
               kbuf, vbuf, *, seqs, kr, scale):
    halo = NA_HALO_ROWS * GRID_W
    blk = NA_BLOCK_ROWS * GRID_W
    for buf, p, c, n in ((kbuf, kp_ref, kc_ref, kn_ref), (vbuf, vp_ref, vc_ref, vn_ref)):
        buf[0:halo, :] = p[...]
        buf[halo:halo + blk, :] = c[...]
        buf[halo + blk:, :] = n[...]

    n_pairs = q_ref.shape[1] // LANES
    row_seqs = tuple((n, length // GRID_W) for n, length in seqs)
    loc0, seq_rows = _seq_position(pl.program_id(0) * NA_BLOCK_ROWS, row_seqs)
    lane = lax.broadcasted_iota(jnp.int32, (GRID_W, LANES), 1)
    low = lane < GRID_W

    pair_lanes = [slice(hp * LANES, (hp + 1) * LANES) for hp in range(n_pairs)]

    def geometry(j):
        r_loc = loc0 + j
        r_start = jnp.clip(r_loc - kr // 2, 0, seq_rows - kr)
        dr0 = r_start - r_loc + (kr - 1)
        koff = pl.multiple_of((r_start - loc0 + NA_HALO_ROWS) * GRID_W, GRID_W)
        qoff = pl.multiple_of(j * GRID_W, GRID_W)
        return qoff, koff, dr0

    def score_pass(qoff, koff):
        scores = []
        for lanes in pair_lanes:
            q2 = q_ref[pl.ds(qoff, GRID_W), lanes].astype(F32) * scale
            qq = jnp.concatenate([jnp.where(low, q2, 0.0), jnp.where(low, 0.0, q2)], axis=0)
            scores.append(_dot_nt(qq.astype(BF16), kbuf[pl.ds(koff, kr * GRID_W), lanes]))
        return scores

    def softmax_pass(scores, dr0):
        probs = []
        for hp, s in enumerate(scores):
            s = s + jnp.concatenate([tab_ref[hp, dr0 + 2 * t] for t in range(kr // 2)], axis=1)
            probs.append(jnp.exp(s - jnp.max(s, axis=-1, keepdims=True)).astype(BF16))
        return probs

    ones = jnp.ones((kr * GRID_W, LANES), BF16)

    def value_pass(probs, qoff, koff):
        for lanes, p in zip(pair_lanes, probs):
            values = jnp.concatenate([vbuf[pl.ds(koff, kr * GRID_W), lanes], ones], axis=1)
            r = _dot(p, values)
            r = r[:, :LANES] / r[:, LANES:]
            o_ref[pl.ds(qoff, GRID_W), lanes] = jnp.where(low, r[:GRID_W], r[GRID_W:]).astype(o_ref.dtype)

    def rows(jj, carry):
        geo = [geometry(jj * NA_ROWS_PER_ITER + r) for r in range(NA_ROWS_PER_ITER)]
        scores = [score_pass(qoff, koff) for qoff, koff, _ in geo]
        for (qoff, koff, dr0), sc in zip(geo, scores):
            value_pass(softmax_pass(sc, dr0), qoff, koff)
        return carry

    lax.fori_loop(0, NA_BLOCK_ROWS // NA_ROWS_PER_ITER, rows, 0)


def _neighbourhood_attention(z, table, seqs, q_col, k_col, v_col, width, head_dim):
    T = z.shape[0]
    kr = table.shape[1] // 2 + 1
    assert head_dim * 2 == LANES and GRID_W * 2 == LANES and kr % 2 == 0
    assert kr == NA_BLOCK_ROWS and NA_HALO_ROWS >= kr // 2
    for _, length in seqs:
        assert length % (NA_BLOCK_ROWS * GRID_W) == 0 and length // GRID_W >= kr
    blk = NA_BLOCK_ROWS * GRID_W
    halo = NA_HALO_ROWS * GRID_W
    halo_per_blk = blk // halo
    last_halo = T // halo - 1

    def cur(col):
        return pl.BlockSpec((blk, width), lambda i: (i, col))

    def prev(col):
        return pl.BlockSpec((halo, width), lambda i: (jnp.maximum(i * halo_per_blk - 1, 0), col))

    def nxt(col):
        return pl.BlockSpec((halo, width),
                            lambda i: (jnp.minimum((i + 1) * halo_per_blk, last_halo), col))

    return pl.pallas_call(
        functools.partial(_na_kernel, seqs=seqs, kr=kr, scale=head_dim ** -0.5),
        grid=(T // blk,),
        in_specs=[cur(q_col), prev(k_col), cur(k_col), nxt(k_col),
                  prev(v_col), cur(v_col), nxt(v_col),
                  pl.BlockSpec(table.shape, lambda i: (0, 0, 0, 0))],
        out_specs=pl.BlockSpec((blk, width), lambda i: (i, 0)),
        out_shape=jax.ShapeDtypeStruct((T, width), BF16),
        scratch_shapes=[pltpu.VMEM((blk + 2 * halo, width), BF16),
                        pltpu.VMEM((blk + 2 * halo, width), BF16)],
        compiler_params=_params("parallel"),
        name="neighbourhood_attention",
    )(z, z, z, z, z, z, z, table)


def _xattn_kernel(q_ref, k_ref, v_ref, o_ref, *, scale):
    hd = q_ref.shape[1] // XA_HEADS
    for h in range(XA_HEADS):
        lanes = slice(h * hd, (h + 1) * hd)
        s = _dot_nt(q_ref[:, lanes], k_ref[0, :, lanes]) * scale
        m = jnp.max(s, axis=-1, keepdims=True)
        p = jnp.exp(s - m)
        l = jnp.sum(p, axis=-1, keepdims=True)
        o_ref[:, lanes] = (_dot(p.astype(BF16), v_ref[0, :, lanes]) / l).astype(o_ref.dtype)


def _cross_attention(q, kv, seqs, *, tm):
    T, D = q.shape
    n_mem = kv.shape[1]
    (n_a, len_a), (_, len_b) = seqs
    assert len_a % tm == 0 and len_b % tm == 0 and (D // XA_HEADS) % LANES == 0
    split = n_a * len_a

    def seq_of(i):
        t0 = i * tm
        return jnp.where(t0 < split, t0 // len_a, n_a + (t0 - split) // len_b)

    return pl.pallas_call(
        functools.partial(_xattn_kernel, scale=(D // XA_HEADS) ** -0.5),
        grid=(T // tm,),
        in_specs=[pl.BlockSpec((tm, D), lambda i: (i, 0)),
                  pl.BlockSpec((1, n_mem, D), lambda i: (seq_of(i), 0, 0)),
                  pl.BlockSpec((1, n_mem, D), lambda i: (seq_of(i), 0, 1))],
        out_specs=pl.BlockSpec((tm, D), lambda i: (i, 0)),
        out_shape=jax.ShapeDtypeStruct((T, D), BF16),
        compiler_params=_params("parallel"),
        name="cross_attention",
    )(q, kv, kv)


def _ffn_kernel(x_ref, g_ref, w1_ref, w2_ref, o_ref, n_ref):
    @pl.when(pl.program_id(1) == 0)
    def _():
        x = x_ref[...]
        n_ref[...] = _rms(x, g_ref[...]).astype(n_ref.dtype)
        o_ref[...] = x

    h = jnp.maximum(_dot(n_ref[...], w1_ref[...]), 0.0)
    o_ref[...] += _dot((h * h).astype(BF16), w2_ref[...])


def _ffn(x, g, w1, w2, layer, *, tm, tf):
    T, D = x.shape
    F = w1.shape[2]
    return pl.pallas_call(
        _ffn_kernel,
        grid=(T // tm, F // tf),
        in_specs=[pl.BlockSpec((tm, D), lambda i, f: (i, 0)),
                  pl.BlockSpec((1, D), lambda i, f: (0, 0)),
                  _layer_weight(layer, (D, tf), lambda i, f: (0, f)),
                  _layer_weight(layer, (tf, D), lambda i, f: (f, 0))],
        out_specs=pl.BlockSpec((tm, D), lambda i, f: (i, 0)),
        out_shape=jax.ShapeDtypeStruct((T, D), F32),
        scratch_shapes=[pltpu.VMEM((tm, D), BF16)],
        compiler_params=_params("parallel", "arbitrary"),
        name="ffn",
    )(x, g.reshape(1, D), w1, w2)


def _rmsnorm_kernel(x_ref, g_ref, o_ref):
    o_ref[...] = _rms(x_ref[...], g_ref[...])


def _rmsnorm(x, g, row0, rows, *, tm):
    D = x.shape[1]
    assert row0 % tm == 0 and rows % tm == 0
    first = row0 // tm
    return pl.pallas_call(
        _rmsnorm_kernel,
        grid=(rows // tm,),
        in_specs=[pl.BlockSpec((tm, D), lambda i: (first + i, 0)), pl.BlockSpec((1, D), lambda i: (0, 0))],
        out_specs=pl.BlockSpec((tm, D), lambda i: (i, 0)),
        out_shape=jax.ShapeDtypeStruct((rows, D), F32),
        compiler_params=_params("parallel"),
        name="final_rmsnorm",
    )(x, g.reshape(1, D))


def _tiles(T, seqs, sgu_chunk):
    seq_gcd = math.gcd(*[length for _, length in seqs])
    group_gcd = math.gcd(*[n * length for n, length in seqs])
    return dict(
        square=_tile(group_gcd, 512),
        merge=_tile(T, 1024),
        final=_tile(seq_gcd, 1024),
        conv=_tile(seq_gcd, 512, CONV_ROWS),
        sgu=_tile(T, 512, sgu_chunk),
        xattn=_tile(seq_gcd, 512),
        ffn=_tile(T, 512),
    )


def kernel(x_prompt, x_sample, mem_prompt, mem_sample, norm_mix, w_in, conv_dw, conv_b, conv_ln_g, conv_ln_b, sgu_ln_g, sgu_ln_b, sgu_w, sgu_b, na_rpb, w_br_conv, w_br_sgu, w_br_na, w_out, norm_xa, norm_mem, xa_wq, xa_wk, xa_wv, xa_wo, norm_ffn, ffn_w1, ffn_w2, norm_final):
    depth, D, d_in = w_in.shape
    Bp, Sp, _ = x_prompt.shape
    Bs, Ss, _ = x_sample.shape
    seqs = ((Bp, Sp), (Bs, Ss))
    T = Bp * Sp + Bs * Ss
    d_conv = conv_dw.shape[-1]
    d_sgu = sgu_ln_g.shape[-1]
    d_na = w_br_na.shape[1]
    na_heads = na_rpb.shape[1]
    na_cols = (na_rpb.shape[3] + 1) // 2
    d_ff = ffn_w1.shape[-1]
    n_mem = mem_prompt.shape[1]
    assert d_conv == d_sgu == d_na, "column blocks of the input projection share one width"
    width = d_conv
    gate_col0 = 7 * width
    assert d_in == gate_col0 + 3 * D and width % LANES == 0

    tiles = _tiles(T, seqs, sgu_w.shape[2])
    tn_d = _tile(D, 1024)
    assert gate_col0 % tn_d == 0

    x_parts = (x_prompt.reshape(Bp * Sp, D), x_sample.reshape(Bs * Ss, D))
    mem =jnp.concatenate([mem_prompt.reshape(Bp * n_mem, D), mem_sample.reshape(Bs * n_mem, D)], axis=0)

    w_in, w_br_conv, w_br_sgu, w_br_na, w_out, xa_wq, xa_wo, ffn_w1, ffn_w2 = (
        w.astype(BF16) for w in (w_in, w_br_conv, w_br_sgu, w_br_na, w_out, xa_wq, xa_wo, ffn_w1, ffn_w2))
    w_kv = jnp.concatenate([xa_wk, xa_wv], axis=2).astype(BF16)

    for l in range(depth):
        z = _norm_matmul(x_parts, norm_mix[l], w_in, l)
        h_conv = _conv_module(z, conv_dw[l], conv_b[l], conv_ln_g[l], conv_ln_b[l], seqs, tt=tiles["conv"])
        sgu_bias = jnp.broadcast_to(sgu_b[l][:, :, None], sgu_b.shape[1:] + (d_sgu // sgu_w.shape[1],))
        h_sgu = _sgu(z, sgu_ln_g[l], sgu_ln_b[l], sgu_w[l].astype(BF16), sgu_bias, 2, 3, tt=tiles["sgu"])
        h_na = _neighbourhood_attention(z, _na_bias_table(na_rpb[l], na_cols), seqs, 4, 5, 6,
                                        width, d_na // na_heads)
        m = _merge(h_conv, h_sgu, h_na, w_br_conv, w_br_sgu, w_br_na, l, z, gate_col0,
                   tm=tiles["merge"], tn=tn_d)
        x = _matmul_residual(m, w_out, l, x_parts, tm=tiles["square"], tn=D)
        q = _norm_matmul((x,), norm_xa[l], xa_wq, l)
        kv = _norm_matmul((mem,), norm_mem[l], w_kv, l).reshape(Bp + Bs, n_mem, 2 * D)
        o = _cross_attention(q, kv, seqs, tm=tiles["xattn"])
        x = _matmul_residual(o, xa_wo, l, (x,), tm=tiles["square"], tn=D)
        x = _ffn(x, norm_ffn[l], ffn_w1, ffn_w2, l, tm=tiles["ffn"], tf=_tile(d_ff, 1024))
        x_parts = (x,)

    y_prompt = _rmsnorm(x, norm_final, 0, Bp * Sp, tm=tiles["final"]).reshape(Bp, Sp, D)
    y_sample = _rmsnorm(x, norm_final, Bp * Sp, Bs * Ss, tm=tiles["final"]).reshape(Bs, Ss, D)
    return (y_prompt, y_sample)
```

```python
import functools
import math

import jax
import jax.numpy as jnp
from jax import lax
from jax.experimental import pallas as pl
from jax.experimental.pallas import tpu as pltpu

F32 = jnp.float32
BF16 = jnp.bfloat16
EPS = 1e-6
MASKED = -1e30

V7X_VMEM_LIMIT_BYTES = 60000 * 1024
VMEM_PLAN_BYTES = V7X_VMEM_LIMIT_BYTES - 4 * 1024 * 1024
LANES = 128
V7X_MXU_COLUMNS = 256

GRID_W = 64
XA_HEADS = 4
CONV_HALO = 16
CONV_ROWS = 64
NA_BLOCK_ROWS = 8
NA_HALO_ROWS = 4
NA_ROWS_PER_ITER = 4


def _params(*semantics):
    return pltpu.CompilerParams(dimension_semantics=semantics,
                                vmem_limit_bytes=V7X_VMEM_LIMIT_BYTES)


def _tile(n, target, multiple=LANES):
    t = min(n, target)
    t -= t % multiple
    while t > multiple and n % t:
        t -= multiple
    assert t > 0 and n % t == 0, (n, target, multiple)
    return t


def _rms(x, g):
    ms = jnp.mean(x * x, axis=-1, keepdims=True)
    return x * lax.rsqrt(ms + EPS) * g


def _layer_norm(x, g, b):
    mu = jnp.mean(x, axis=-1, keepdims=True)
    d = x - mu
    var = jnp.mean(d * d, axis=-1, keepdims=True)
    return d * lax.rsqrt(var + EPS) * g + b


def _dot(a, b):
    return jnp.dot(a, b, preferred_element_type=F32)


def _dot_nt(a, b):
    return lax.dot_general(a, b, (((1,), (1,)), ((), ())), preferred_element_type=F32)


def _layer_weight(layer, block, index):
    return pl.BlockSpec((None,) + block, lambda *grid: (layer,) + index(*grid))


def _row_part_specs(parts, tm, cols, col_index):
    specs, first = [], 0
    for p in parts:
        n = p.shape[0] // tm
        assert n * tm == p.shape[0]
        specs.append(pl.BlockSpec(
            (tm, cols), lambda i, j, first=first, n=n: (jnp.clip(i - first, 0, n - 1), col_index(j))))
        first += n
    return specs


def _part_tiles(parts, tm):
    return tuple(p.shape[0] // tm for p in parts)


def _with_part(refs, part_tiles, i, body):
    if len(refs) == 1:
        body(refs[0])
        return
    start = 0
    for ref, n in zip(refs, part_tiles):
        pl.when((i >= start) & (i < start + n))(functools.partial(body, ref))
        start += n


def _norm_matmul_kernel(*refs, part_tiles):
    x_refs = refs[:len(part_tiles)]
    g_ref, w_ref, o_ref, n_ref = refs[len(part_tiles):]

    def normalise(x_ref):
        n_ref[...] = _rms(x_ref[...], g_ref[...]).astype(n_ref.dtype)

    @pl.when(pl.program_id(1) == 0)
    def _():
        _with_part(x_refs, part_tiles, pl.program_id(0), normalise)

    o_ref[...] = _dot(n_ref[...], w_ref[...]).astype(o_ref.dtype)


def _norm_matmul_vmem_bytes(n_parts, tm, D, tn):
    return n_parts * 2 * tm * D * 4 + 2 * D * tn * 2 + 2 * tm * tn * 2 + tm * D * 2 + tm * tn * 4


def _norm_matmul_tiles(n_parts, T, rows_gcd, D, N):
    best = None
    for tm in (1024, 512, 256):
        if rows_gcd % tm:
            continue
        for tn in range(V7X_MXU_COLUMNS, N + 1, V7X_MXU_COLUMNS):
            if N % tn == 0 and _norm_matmul_vmem_bytes(n_parts, tm, D, tn) <= VMEM_PLAN_BYTES:
                steps = (T // tm) * (N // tn)
                if best is None or steps < best[0]:
                    best = (steps, tm, tn)
    assert best is not None, (n_parts, T, rows_gcd, D, N)
    return best[1:]


def _norm_matmul(x_parts, g, w, layer):
    T = sum(p.shape[0] for p in x_parts)
    D = x_parts[0].shape[1]
    N = w.shape[2]
    tm, tn = _norm_matmul_tiles(len(x_parts), T, math.gcd(*[p.shape[0] for p in x_parts]), D, N)
    return pl.pallas_call(
        functools.partial(_norm_matmul_kernel, part_tiles=_part_tiles(x_parts, tm)),
        grid=(T // tm, N // tn),
        in_specs=_row_part_specs(x_parts, tm, D, lambda j: 0) + [
            pl.BlockSpec((1, D), lambda i, j: (0, 0)),
            _layer_weight(layer, (D, tn), lambda i, j: (0, j))],
        out_specs=pl.BlockSpec((tm, tn), lambda i, j: (i, j)),
        out_shape=jax.ShapeDtypeStruct((T, N), BF16),
        scratch_shapes=[pltpu.VMEM((tm, D), BF16)],
        compiler_params=_params("parallel", "arbitrary"),
        name="norm_matmul",
    )(*x_parts, g.reshape(1, D), w)


def _matmul_residual_kernel(a_ref, w_ref, *refs, part_tiles):
    x_refs, o_ref = refs[:-1], refs[-1]
    update = _dot(a_ref[...], w_ref[...])

    def add_residual(x_ref):
        o_ref[...] = x_ref[...] + update

    _with_part(x_refs, part_tiles, pl.program_id(0), add_residual)


def _matmul_residual(a, w, layer, x_parts, *, tm, tn):
    T, K = a.shape
    N = w.shape[2]
    return pl.pallas_call(
        functools.partial(_matmul_residual_kernel, part_tiles=_part_tiles(x_parts, tm)),
        grid=(T // tm, N // tn),
        in_specs=[pl.BlockSpec((tm, K), lambda i, j: (i, 0)),
                  _layer_weight(layer, (K, tn), lambda i, j: (0, j))]
        + _row_part_specs(x_parts, tm, tn, lambda j: j),
        out_specs=pl.BlockSpec((tm, tn), lambda i, j: (i, j)),
        out_shape=jax.ShapeDtypeStruct((T, N), F32),
        compiler_params=_params("parallel", "arbitrary"),
        name="matmul_residual",
    )(a, w, *x_parts)


def _merge_kernel(hc_ref, hs_ref, hn_ref, wc_ref, ws_ref, wn_ref, g0_ref, g1_ref, g2_ref, o_ref):
    def gate(g_ref):
        return jax.nn.sigmoid(g_ref[...].astype(F32))

    m = gate(g0_ref) * _dot(hc_ref[...], wc_ref[...])
    m += gate(g1_ref) * _dot(hs_ref[...], ws_ref[...])
    m += gate(g2_ref) * _dot(hn_ref[...], wn_ref[...])
    o_ref[...] = m.astype(o_ref.dtype)


def _merge(hc, hs, hn, wc, ws, wn, layer, z, gate_col0, *, tm, tn):
    T = hc.shape[0]
    D = wc.shape[2]
    gate_block0 = gate_col0 // tn
    blocks_per_gate = D // tn

    def h_spec(h):
        return pl.BlockSpec((tm, h.shape[1]), lambda i, j: (i, 0))

    def w_spec(w):
        return _layer_weight(layer, (w.shape[1], tn), lambda i, j: (0, j))

    def g_spec(b):
        return pl.BlockSpec((tm, tn), lambda i, j: (i, gate_block0 + b * blocks_per_gate + j))

    return pl.pallas_call(
        _merge_kernel,
        grid=(T // tm, D // tn),
        in_specs=[h_spec(hc), h_spec(hs), h_spec(hn), w_spec(wc), w_spec(ws), w_spec(wn),
                  g_spec(0), g_spec(1), g_spec(2)],
        out_specs=pl.BlockSpec((tm, tn), lambda i, j: (i, j)),
        out_shape=jax.ShapeDtypeStruct((T, D), BF16),
        compiler_params=_params("parallel", "arbitrary"),
        name="branch_merge",
    )(hc, hs, hn, wc, ws, wn, z, z, z)


def _seq_position(t0, seqs):
    (n_a, len_a), (_, len_b) = seqs
    split = n_a * len_a
    in_a = t0 < split
    pos = jnp.where(in_a, lax.rem(t0, len_a), lax.rem(t0 - split, len_b))
    return pos, jnp.where(in_a, len_a, len_b)


def _conv_kernel(a_ref, b_ref, ap_ref, bp_ref, an_ref, bn_ref, w_ref, cb_ref, g_ref, beta_ref,
                 o_ref, hs_ref, y_ref, *, tt, seqs):
    width = w_ref.shape[0]
    C = a_ref.shape[1]
    first_tap = CONV_HALO - width // 2

    def glu(a, b):
        return a[...].astype(F32) * jax.nn.sigmoid(b[...].astype(F32))

    pos, seq_len = _seq_position(pl.program_id(0) * tt, seqs)
    hs_ref[0:CONV_HALO, :] = jnp.where(pos == 0, 0.0, glu(ap_ref, bp_ref))
    hs_ref[CONV_HALO:CONV_HALO + tt, :] = glu(a_ref, b_ref)
    hs_ref[CONV_HALO + tt:, :] = jnp.where(pos + tt == seq_len, 0.0, glu(an_ref, bn_ref))

    span = CONV_ROWS + 2 * CONV_HALO

    def rows(r, carry):
        r0 = pl.multiple_of(r * CONV_ROWS, CONV_ROWS)
        for c in range(C // LANES):
            lanes = slice(c * LANES, (c + 1) * LANES)
            win = hs_ref[pl.ds(r0, span), lanes]
            acc = jnp.zeros((CONV_ROWS, LANES), F32)
            for phase in range(8):
                taps = [j for j in range(width) if (first_tap + j) % 8 == phase]
                if not taps:
                    continue
                shifted = win if phase == 0 else pltpu.roll(win, span - phase, axis=0)
                for j in taps:
                    a8 = first_tap + j - phase
                    acc = acc + w_ref[j:j + 1, lanes] * shifted[a8:a8 + CONV_ROWS]
            y_ref[pl.ds(r0, CONV_ROWS), lanes] = acc + cb_ref[:, lanes]
        return carry

    lax.fori_loop(0, tt // CONV_ROWS, rows, 0)

    y = _layer_norm(y_ref[...], g_ref[...], beta_ref[...])
    o_ref[...] = (y * jax.nn.sigmoid(y)).astype(o_ref.dtype)


def _conv_module(z, conv_dw, conv_b, ln_g, ln_b, seqs, *, tt):
    T = z.shape[0]
    width, C = conv_dw.shape
    assert width // 2 <= CONV_HALO and tt % CONV_ROWS == 0 and tt % CONV_HALO == 0
    assert 7 + ((CONV_HALO - width // 2 + width - 1) // 8) * 8 + CONV_ROWS <= CONV_ROWS + 2 * CONV_HALO
    halo_per_tile = tt // CONV_HALO
    last_halo = T // CONV_HALO - 1

    def cur(col):
        return pl.BlockSpec((tt, C), lambda i: (i, col))

    def prev(col):
        return pl.BlockSpec((CONV_HALO, C), lambda i: (jnp.maximum(i * halo_per_tile - 1, 0), col))

    def nxt(col):
        return pl.BlockSpec((CONV_HALO, C),
                            lambda i: (jnp.minimum((i + 1) * halo_per_tile, last_halo), col))

    def vec():
        return pl.BlockSpec((1, C), lambda i: (0, 0))

    return pl.pallas_call(
        functools.partial(_conv_kernel, tt=tt, seqs=seqs),
        grid=(T // tt,),
        in_specs=[cur(0), cur(1), prev(0), prev(1), nxt(0), nxt(1),
                  pl.BlockSpec((width, C), lambda i: (0, 0)), vec(), vec(), vec()],
        out_specs=pl.BlockSpec((tt, C), lambda i: (i, 0)),
        out_shape=jax.ShapeDtypeStruct((T, C), BF16),
        scratch_shapes=[pltpu.VMEM((tt + 2 * CONV_HALO, C), F32), pltpu.VMEM((tt, C), F32)],
        compiler_params=_params("parallel"),
        name="conv_module",
    )(z, z, z, z, z, z, conv_dw, conv_b.reshape(1, C), ln_g.reshape(1, C), ln_b.reshape(1, C))


def _sgu_kernel(u_ref, v_ref, g_ref, beta_ref, w_ref, bias_ref, o_ref, vn_ref):
    groups, chunk, _ = w_ref.shape
    tt, C = v_ref.shape
    gd = C // groups
    vn_ref[...] = _layer_norm(v_ref[...].astype(F32), g_ref[...], beta_ref[...]).astype(vn_ref.dtype)
    for n in range(tt // chunk):
        rows = slice(n * chunk, (n + 1) * chunk)
        for g in range(groups):
            lanes = slice(g * gd, (g + 1) * gd)
            s = _dot(w_ref[g], vn_ref[rows, lanes]) + bias_ref[g]
            o_ref[rows, lanes] = (u_ref[rows, lanes].astype(F32) * s).astype(o_ref.dtype)


def _sgu(z, ln_g, ln_b, sgu_w, sgu_bias, u_col, v_col, *, tt):
    T = z.shape[0]
    groups, chunk, _ = sgu_w.shape
    C = ln_g.shape[0]
    gd = C // groups
    assert gd % LANES == 0 and tt % chunk == 0

    def vec():
        return pl.BlockSpec((1, C), lambda i: (0, 0))

    return pl.pallas_call(
        _sgu_kernel,
        grid=(T // tt,),
        in_specs=[pl.BlockSpec((tt, C), lambda i: (i, u_col)),
                  pl.BlockSpec((tt, C), lambda i: (i, v_col)),
                  vec(), vec(),
                  pl.BlockSpec((groups, chunk, chunk), lambda i: (0, 0, 0)),
                  pl.BlockSpec((groups, chunk, gd), lambda i: (0, 0, 0))],
        out_specs=pl.BlockSpec((tt, C), lambda i: (i, 0)),
        out_shape=jax.ShapeDtypeStruct((T, C), BF16),
        scratch_shapes=[pltpu.VMEM((tt, C), BF16)],
        compiler_params=_params("parallel"),
        name="sgu",
    )(z, z, ln_g.reshape(1, C), ln_b.reshape(1, C), sgu_w, sgu_bias)


def _na_bias_table(rpb, na_cols):
    H, n_dr, _ = rpb.shape
    c = jnp.arange(GRID_W)
    c_start = jnp.clip(c - na_cols // 2, 0, GRID_W - na_cols)
    kc = jnp.arange(GRID_W)
    valid = (kc[None, :] >= c_start[:, None]) & (kc[None, :] < c_start[:, None] + na_cols)
    dc = jnp.clip(kc[None, :] - c[:, None] + (na_cols - 1), 0, 2 * na_cols - 2)
    dense = jnp.where(valid[None, None], rpb[:, :, dc].astype(F32), MASKED)
    pairs = jnp.concatenate([dense[:, :-1], dense[:, 1:]], axis=-1)
    pairs = pairs.reshape(H // 2, 2, n_dr - 1, GRID_W, 2 * GRID_W).transpose(0, 2, 1, 3, 4)
    return pairs.reshape(H // 2, n_dr - 1, 2 * GRID_W, 2 * GRID_W)


def _na_kernel(q_ref, kp_ref, kc_ref, kn_ref, vp_ref, vc_ref, vn_ref, tab_ref, o_ref,
               kbuf, vbuf, *, seqs, kr, scale):
    halo = NA_HALO_ROWS * GRID_W
    blk = NA_BLOCK_ROWS * GRID_W
    for buf, p, c, n in ((kbuf, kp_ref, kc_ref, kn_ref), (vbuf, vp_ref, vc_ref, vn_ref)):
        buf[0:halo, :] = p[...]
        buf[halo:halo + blk, :] = c[...]
        buf[halo + blk:, :] = n[...]

    n_pairs = q_ref.shape[1] // LANES
    row_seqs = tuple((n, length // GRID_W) for n, length in seqs)
    loc0, seq_rows = _seq_position(pl.program_id(0) * NA_BLOCK_ROWS, row_seqs)
    lane = lax.broadcasted_iota(jnp.int32, (GRID_W, LANES), 1)
    low = lane < GRID_W

    pair_lanes = [slice(hp * LANES, (hp + 1) * LANES) for hp in range(n_pairs)]

    def geometry(j):
        r_loc = loc0 + j
        r_start = jnp.clip(r_loc - kr // 2, 0, seq_rows - kr)
        dr0 = r_start - r_loc + (kr - 1)
        koff = pl.multiple_of((r_start - loc0 + NA_HALO_ROWS) * GRID_W, GRID_W)
        qoff = pl.multiple_of(j * GRID_W, GRID_W)
        return qoff, koff, dr0

    def score_pass(qoff, koff):
        scores = []
        for lanes in pair_lanes:
            q2 = q_ref[pl.ds(qoff, GRID_W), lanes].astype(F32) * scale
            qq = jnp.concatenate([jnp.where(low, q2, 0.0), jnp.where(low, 0.0, q2)], axis=0)
            scores.append(_dot_nt(qq.astype(BF16), kbuf[pl.ds(koff, kr * GRID_W), lanes]))
        return scores

    def softmax_pass(scores, dr0):
        probs = []
        for hp, s in enumerate(scores):
            s = s + jnp.concatenate([tab_ref[hp, dr0 + 2 * t] for t in range(kr // 2)], axis=1)
            probs.append(jnp.exp(s - jnp.max(s, axis=-1, keepdims=True)).astype(BF16))
        return probs

    ones = jnp.ones((kr * GRID_W, LANES), BF16)

    def value_pass(probs, qoff, koff):
        for lanes, p in zip(pair_lanes, probs):
            values = jnp.concatenate([vbuf[pl.ds(koff, kr * GRID_W), lanes], ones], axis=1)
            r = _dot(p, values)
            r = r[:, :LANES] / r[:, LANES:]
            o_ref[pl.ds(qoff, GRID_W), lanes] = jnp.where(low, r[:GRID_W], r[GRID_W:]).astype(o_ref.dtype)

    def rows(jj, carry):
        geo = [geometry(jj * NA_ROWS_PER_ITER + r) for r in range(NA_ROWS_PER_ITER)]
        scores = [score_pass(qoff, koff) for qoff, koff, _ in geo]
        for (qoff, koff, dr0), sc in zip(geo, scores):
            value_pass(softmax_pass(sc, dr0), qoff, koff)
        return carry

    lax.fori_loop(0, NA_BLOCK_ROWS // NA_ROWS_PER_ITER, rows, 0)


def _neighbourhood_attention(z, table, seqs, q_col, k_col, v_col, width, head_dim):
    T = z.shape[0]
    kr = table.shape[1] // 2 + 1
    assert head_dim * 2 == LANES and GRID_W * 2 == LANES and kr % 2 == 0
    assert kr == NA_BLOCK_ROWS and NA_HALO_ROWS >= kr // 2
    for _, length in seqs:
        assert length % (NA_BLOCK_ROWS * GRID_W) == 0 and length // GRID_W >= kr
    blk = NA_BLOCK_ROWS * GRID_W
    halo = NA_HALO_ROWS * GRID_W
    halo_per_blk = blk // halo
    last_halo = T // halo - 1

    def cur(col):
        return pl.BlockSpec((blk, width), lambda i: (i, col))

    def prev(col):
        return pl.BlockSpec((halo, width), lambda i: (jnp.maximum(i * halo_per_blk - 1, 0), col))

    def nxt(col):
        return pl.BlockSpec((halo, width),
                            lambda i: (jnp.minimum((i + 1) * halo_per_blk, last_halo), col))

    return pl.pallas_call(
        functools.partial(_na_kernel, seqs=seqs, kr=kr, scale=head_dim ** -0.5),
        grid=(T // blk,),
        in_specs=[cur(q_col), prev(k_col), cur(k_col), nxt(k_col),
                  prev(v_col), cur(v_col), nxt(v_col),
                  pl.BlockSpec(table.shape, lambda i: (0, 0, 0, 0))],
        out_specs=pl.BlockSpec((blk, width), lambda i: (i, 0)),
        out_shape=jax.ShapeDtypeStruct((T, width), BF16),
        scratch_shapes=[pltpu.VMEM((blk + 2 * halo, width), BF16),
                        pltpu.VMEM((blk + 2 * halo, width), BF16)],
        compiler_params=_params("parallel"),
        name="neighbourhood_attention",
    )(z, z, z, z, z, z, z, table)


def _xattn_kernel(q_ref, k_ref, v_ref, o_ref, *, scale):
    hd = q_ref.shape[1] // XA_HEADS
    head_lanes = [slice(h * hd, (h + 1) * hd) for h in range(XA_HEADS)]
    scores = [_dot_nt(q_ref[:, lanes], k_ref[0, :, lanes]) for lanes in head_lanes]
    probs, inv_sums = [], []
    for s in scores:
        s = s * scale
        p = jnp.exp(s - jnp.max(s, axis=-1, keepdims=True))
        inv_sums.append(1.0 / jnp.sum(p, axis=-1, keepdims=True))
        probs.append(p.astype(BF16))
    for lanes, p, inv in zip(head_lanes, probs, inv_sums):
        o_ref[:, lanes] = (_dot(p, v_ref[0, :, lanes]) * inv).astype(o_ref.dtype)


def _cross_attention(q, kv, seqs, *, tm):
    T, D = q.shape
    n_mem = kv.shape[1]
    (n_a, len_a), (_, len_b) = seqs
    assert len_a % tm == 0 and len_b % tm == 0 and (D // XA_HEADS) % LANES == 0
    split = n_a * len_a

    def seq_of(i):
        t0 = i * tm
        return jnp.where(t0 < split, t0 // len_a, n_a + (t0 - split) // len_b)

    return pl.pallas_call(
        functools.partial(_xattn_kernel, scale=(D // XA_HEADS) ** -0.5),
        grid=(T // tm,),
        in_specs=[pl.BlockSpec((tm, D), lambda i: (i, 0)),
                  pl.BlockSpec((1, n_mem, D), lambda i: (seq_of(i), 0, 0)),
                  pl.BlockSpec((1, n_mem, D), lambda i: (seq_of(i), 0, 1))],
        out_specs=pl.BlockSpec((tm, D), lambda i: (i, 0)),
        out_shape=jax.ShapeDtypeStruct((T, D), BF16),
        compiler_params=_params("parallel"),
        name="cross_attention",
    )(q, kv, kv)


def _ffn_kernel(x_ref, g_ref, w1_ref, w2_ref, o_ref, n_ref):
    @pl.when(pl.program_id(1) == 0)
    def _():
        x = x_ref[...]
        n_ref[...] = _rms(x, g_ref[...]).astype(n_ref.dtype)
        o_ref[...] = x

    h = jnp.maximum(_dot(n_ref[...], w1_ref[...]), 0.0)
    o_ref[...] += _dot((h * h).astype(BF16), w2_ref[...])


def _ffn(x, g, w1, w2, layer, *, tm, tf):
    T, D = x.shape
    F = w1.shape[2]
    return pl.pallas_call(
        _ffn_kernel,
        grid=(T // tm, F // tf),
        in_specs=[pl.BlockSpec((tm, D), lambda i, f: (i, 0)),
                  pl.BlockSpec((1, D), lambda i, f: (0, 0)),
                  _layer_weight(layer, (D, tf), lambda i, f: (0, f)),
                  _layer_weight(layer, (tf, D), lambda i, f: (f, 0))],
        out_specs=pl.BlockSpec((tm, D), lambda i, f: (i, 0)),
        out_shape=jax.ShapeDtypeStruct((T, D), F32),
        scratch_shapes=[pltpu.VMEM((tm, D), BF16)],
        compiler_params=_params("parallel", "arbitrary"),
        name="ffn",
    )(x, g.reshape(1, D), w1, w2)


def _rmsnorm_kernel(x_ref, g_ref, o_ref):
    o_ref[...] = _rms(x_ref[...], g_ref[...])


def _rmsnorm(x, g, row0, rows, *, tm):
    D = x.shape[1]
    assert row0 % tm == 0 and rows % tm == 0
    first = row0 // tm
    return pl.pallas_call(
        _rmsnorm_kernel,
        grid=(rows // tm,),
        in_specs=[pl.BlockSpec((tm, D), lambda i: (first + i, 0)), pl.BlockSpec((1, D), lambda i: (0, 0))],
        out_specs=pl.BlockSpec((tm, D), lambda i: (i, 0)),
        out_shape=jax.ShapeDtypeStruct((rows, D), F32),
        compiler_params=_params("parallel"),
        name="final_rmsnorm",
    )(x, g.reshape(1, D))


def _tiles(T, seqs, sgu_chunk):
    seq_gcd = math.gcd(*[length for _, length in seqs])
    group_gcd = math.gcd(*[n * length for n, length in seqs])
    return dict(
        square=_tile(group_gcd, 512),
        merge=_tile(T, 1024),
        final=_tile(seq_gcd, 1024),
        conv=_tile(seq_gcd, 512, CONV_ROWS),
        sgu=_tile(T, 1024, sgu_chunk),
        xattn=_tile(seq_gcd, 512),
        ffn=_tile(T, 512),
    )


def kernel(x_prompt, x_sample, mem_prompt, mem_sample, norm_mix, w_in, conv_dw, conv_b, conv_ln_g, conv_ln_b, sgu_ln_g, sgu_ln_b, sgu_w, sgu_b, na_rpb, w_br_conv, w_br_sgu, w_br_na, w_out, norm_xa, norm_mem, xa_wq, xa_wk, xa_wv, xa_wo, norm_ffn, ffn_w1, ffn_w2, norm_final):
    depth, D, d_in = w_in.shape
    Bp, Sp, _ = x_prompt.shape
    Bs, Ss, _ = x_sample.shape
    seqs = ((Bp, Sp), (Bs, Ss))
    T = Bp * Sp + Bs * Ss
    d_conv = conv_dw.shape[-1]
    d_sgu = sgu_ln_g.shape[-1]
    d_na = w_br_na.shape[1]
    na_heads = na_rpb.shape[1]
    na_cols = (na_rpb.shape[3] + 1) // 2
    d_ff = ffn_w1.shape[-1]
    n_mem = mem_prompt.shape[1]
    assert d_conv == d_sgu == d_na, "column blocks of the input projection share one width"
    width = d_conv
    gate_col0 = 7 * width
    assert d_in == gate_col0 + 3 * D and width % LANES == 0

    tiles = _tiles(T, seqs, sgu_w.shape[2])
    tn_d = _tile(D, 1024)
    assert gate_col0 % tn_d == 0

    x_parts = (x_prompt.reshape(Bp * Sp, D), x_sample.reshape(Bs * Ss, D))
    mem =jnp.concatenate([mem_prompt.reshape(Bp * n_mem, D), mem_sample.reshape(Bs * n_mem, D)], axis=0)

    w_in, w_br_conv, w_br_sgu, w_br_na, w_out, xa_wq, xa_wo, ffn_w1, ffn_w2 = (
        w.astype(BF16) for w in (w_in, w_br_conv, w_br_sgu, w_br_na, w_out, xa_wq, xa_wo, ffn_w1, ffn_w2))
    w_kv = jnp.concatenate([xa_wk, xa_wv], axis=2).astype(BF16)

    for l in range(depth):
        z = _norm_matmul(x_parts, norm_mix[l], w_in, l)
        h_conv = _conv_module(z, conv_dw[l], conv_b[l], conv_ln_g[l], conv_ln_b[l], seqs, tt=tiles["conv"])
        sgu_bias = jnp.broadcast_to(sgu_b[l][:, :, None], sgu_b.shape[1:] + (d_sgu // sgu_w.shape[1],))
        h_sgu = _sgu(z, sgu_ln_g[l], sgu_ln_b[l], sgu_w[l].astype(BF16), sgu_bias, 2, 3, tt=tiles["sgu"])
        h_na = _neighbourhood_attention(z, _na_bias_table(na_rpb[l], na_cols), seqs, 4, 5, 6,
                                        width, d_na // na_heads)
        m = _merge(h_conv, h_sgu, h_na, w_br_conv, w_br_sgu, w_br_na, l, z, gate_col0,
                   tm=tiles["merge"], tn=tn_d)
        x = _matmul_residual(m, w_out, l, x_parts, tm=tiles["square"], tn=D)
        q = _norm_matmul((x,), norm_xa[l], xa_wq, l)
        kv = _norm_matmul((mem,), norm_mem[l], w_kv, l).reshape(Bp + Bs, n_mem, 2 * D)
        o = _cross_attention(q, kv, seqs, tm=tiles["xattn"])
        x = _matmul_residual(o, xa_wo, l, (x,), tm=tiles["square"], tn=D)
        x = _ffn(x, norm_ffn[l], ffn_w1, ffn_w2, l, tm=tiles["ffn"], tf=_tile(d_ff, 1024))
        x_parts = (x,)

    y_prompt = _rmsnorm(x, norm_final, 0, Bp * Sp, tm=tiles["final"]).reshape(Bp, Sp, D)
    y_sample = _rmsnorm(x, norm_final, Bp * Sp, Bs * Ss, tm=tiles["final"]).reshape(Bs, Ss, D)
    return (y_prompt, y_sample)
```

```python
import functools
import math

import jax
import jax.numpy as jnp
from jax import lax
from jax.experimental import pallas as pl
from jax.experimental.pallas import tpu as pltpu

F32 = jnp.float32
BF16 = jnp.bfloat16
EPS = 1e-6
MASKED = -1e30

V7X_VMEM_LIMIT_BYTES = 60000 * 1024
VMEM_PLAN_BYTES = V7X_VMEM_LIMIT_BYTES - 4 * 1024 * 1024
LANES = 128
V7X_MXU_COLUMNS = 256

GRID_W = 64
XA_HEADS = 4
CONV_HALO = 16
CONV_ROWS = 64
NA_BLOCK_ROWS = 8
NA_HALO_ROWS = 4
NA_ROWS_PER_ITER = 4


def _params(*semantics):
    return pltpu.CompilerParams(dimension_semantics=semantics,
                                vmem_limit_bytes=V7X_VMEM_LIMIT_BYTES)


def _tile(n, target, multiple=LANES):
    t = min(n, target)
    t -= t % multiple
    while t > multiple and n % t:
        t -= multiple
    assert t > 0 and n % t == 0, (n, target, multiple)
    return t


def _rms(x, g):
    ms = jnp.mean(x * x, axis=-1, keepdims=True)
    return x * lax.rsqrt(ms + EPS) * g


def _layer_norm(x, g, b):
    mu = jnp.mean(x, axis=-1, keepdims=True)
    d = x - mu
    var = jnp.mean(d * d, axis=-1, keepdims=True)
    return d * lax.rsqrt(var + EPS) * g + b


def _dot(a, b):
    return jnp.dot(a, b, preferred_element_type=F32)


def _dot_nt(a, b):
    return lax.dot_general(a, b, (((1,), (1,)), ((), ())), preferred_element_type=F32)


def _layer_weight(layer, block, index):
    return pl.BlockSpec((None,) + block, lambda *grid: (layer,) + index(*grid))


def _row_part_specs(parts, tm, cols, col_index):
    specs, first = [], 0
    for p in parts:
        n = p.shape[0] // tm
        assert n * tm == p.shape[0]
        specs.append(pl.BlockSpec(
            (tm, cols), lambda i, j, first=first, n=n: (jnp.clip(i - first, 0, n - 1), col_index(j))))
        first += n
    return specs


def _part_tiles(parts, tm):
    return tuple(p.shape[0] // tm for p in parts)


def _with_part(refs, part_tiles, i, body):
    if len(refs) == 1:
        body(refs[0])
        return
    start = 0
    for ref, n in zip(refs, part_tiles):
        pl.when((i >= start) & (i < start + n))(functools.partial(body, ref))
        start += n


def _norm_matmul_kernel(*refs, part_tiles):
    x_refs = refs[:len(part_tiles)]
    g_ref, w_ref, o_ref, n_ref = refs[len(part_tiles):]

    def normalise(x_ref):
        n_ref[...] = _rms(x_ref[...], g_ref[...]).astype(n_ref.dtype)

    @pl.when(pl.program_id(1) == 0)
    def _():
        _with_part(x_refs, part_tiles, pl.program_id(0), normalise)

    o_ref[...] = _dot(n_ref[...], w_ref[...]).astype(o_ref.dtype)


def _norm_matmul_vmem_bytes(n_parts, tm, D, tn):
    return n_parts * 2 * tm * D * 4 + 2 * D * tn * 2 + 2 * tm * tn * 2 + tm * D * 2 + tm * tn * 4


def _norm_matmul_tiles(n_parts, T, rows_gcd, D, N):
    best = None
    for tm in (1024, 512, 256):
        if rows_gcd % tm:
            continue
        for tn in range(V7X_MXU_COLUMNS, N + 1, V7X_MXU_COLUMNS):
            if N % tn == 0 and _norm_matmul_vmem_bytes(n_parts, tm, D, tn) <= VMEM_PLAN_BYTES:
                steps = (T // tm) * (N // tn)
                if best is None or steps < best[0]:
                    best = (steps, tm, tn)
    assert best is not None, (n_parts, T, rows_gcd, D, N)
    return best[1:]


def _norm_matmul(x_parts, g, w, layer):
    T = sum(p.shape[0] for p in x_parts)
    D = x_parts[0].shape[1]
    N = w.shape[2]
    tm, tn = _norm_matmul_tiles(len(x_parts), T, math.gcd(*[p.shape[0] for p in x_parts]), D, N)
    return pl.pallas_call(
        functools.partial(_norm_matmul_kernel, part_tiles=_part_tiles(x_parts, tm)),
        grid=(T // tm, N // tn),
        in_specs=_row_part_specs(x_parts, tm, D, lambda j: 0) + [
            pl.BlockSpec((1, D), lambda i, j: (0, 0)),
            _layer_weight(layer, (D, tn), lambda i, j: (0, j))],
        out_specs=pl.BlockSpec((tm, tn), lambda i, j: (i, j)),
        out_shape=jax.ShapeDtypeStruct((T, N), BF16),
        scratch_shapes=[pltpu.VMEM((tm, D), BF16)],
        compiler_params=_params("parallel", "arbitrary"),
        name="norm_matmul",
    )(*x_parts, g.reshape(1, D), w)


def _matmul_residual_kernel(a_ref, w_ref, *refs, part_tiles):
    x_refs, o_ref = refs[:-1], refs[-1]
    update = _dot(a_ref[...], w_ref[...])

    def add_residual(x_ref):
        o_ref[...] = x_ref[...] + update

    _with_part(x_refs, part_tiles, pl.program_id(0), add_residual)


def _matmul_residual(a, w, layer, x_parts, *, tm, tn):
    T, K = a.shape
    N = w.shape[2]
    return pl.pallas_call(
        functools.partial(_matmul_residual_kernel, part_tiles=_part_tiles(x_parts, tm)),
        grid=(T // tm, N // tn),
        in_specs=[pl.BlockSpec((tm, K), lambda i, j: (i, 0)),
                  _layer_weight(layer, (K, tn), lambda i, j: (0, j))]
        + _row_part_specs(x_parts, tm, tn, lambda j: j),
        out_specs=pl.BlockSpec((tm, tn), lambda i, j: (i, j)),
        out_shape=jax.ShapeDtypeStruct((T, N), F32),
        compiler_params=_params("parallel", "arbitrary"),
        name="matmul_residual",
    )(a, w, *x_parts)


def _merge_kernel(hc_ref, hs_ref, hn_ref, wc_ref, ws_ref, wn_ref, g0_ref, g1_ref, g2_ref, o_ref):
    def gate(g_ref):
        return jax.nn.sigmoid(g_ref[...].astype(F32))

    m = gate(g0_ref) * _dot(hc_ref[...], wc_ref[...])
    m += gate(g1_ref) * _dot(hs_ref[...], ws_ref[...])
    m += gate(g2_ref) * _dot(hn_ref[...], wn_ref[...])
    o_ref[...] = m.astype(o_ref.dtype)


def _merge(hc, hs, hn, wc, ws, wn, layer, z, gate_col0, *, tm, tn):
    T = hc.shape[0]
    D = wc.shape[2]
    gate_block0 = gate_col0 // tn
    blocks_per_gate = D // tn

    def h_spec(h):
        return pl.BlockSpec((tm, h.shape[1]), lambda i, j: (i, 0))

    def w_spec(w):
        return _layer_weight(layer, (w.shape[1], tn), lambda i, j: (0, j))

    def g_spec(b):
        return pl.BlockSpec((tm, tn), lambda i, j: (i, gate_block0 + b * blocks_per_gate + j))

    return pl.pallas_call(
        _merge_kernel,
        grid=(T // tm, D // tn),
        in_specs=[h_spec(hc), h_spec(hs), h_spec(hn), w_spec(wc), w_spec(ws), w_spec(wn),
                  g_spec(0), g_spec(1), g_spec(2)],
        out_specs=pl.BlockSpec((tm, tn), lambda i, j: (i, j)),
        out_shape=jax.ShapeDtypeStruct((T, D), BF16),
        compiler_params=_params("parallel", "arbitrary"),
        name="branch_merge",
    )(hc, hs, hn, wc, ws, wn, z, z, z)


def _seq_position(t0, seqs):
    (n_a, len_a), (_, len_b) = seqs
    split = n_a * len_a
    in_a = t0 < split
    pos = jnp.where(in_a, lax.rem(t0, len_a), lax.rem(t0 - split, len_b))
    return pos, jnp.where(in_a, len_a, len_b)


def _conv_kernel(a_ref, b_ref, ap_ref, bp_ref, an_ref, bn_ref, w_ref, cb_ref, g_ref, beta_ref,
                 o_ref, hs_ref, y_ref, *, tt, seqs):
    width = w_ref.shape[0]
    C = a_ref.shape[1]
    first_tap = CONV_HALO - width // 2

    def glu(a, b):
        return a[...].astype(F32) * jax.nn.sigmoid(b[...].astype(F32))

    pos, seq_len = _seq_position(pl.program_id(0) * tt, seqs)
    hs_ref[0:CONV_HALO, :] = jnp.where(pos == 0, 0.0, glu(ap_ref, bp_ref))
    hs_ref[CONV_HALO:CONV_HALO + tt, :] = glu(a_ref, b_ref)
    hs_ref[CONV_HALO + tt:, :] = jnp.where(pos + tt == seq_len, 0.0, glu(an_ref, bn_ref))

    span = CONV_ROWS + 2 * CONV_HALO

    def rows(r, carry):
        r0 = pl.multiple_of(r * CONV_ROWS, CONV_ROWS)
        for c in range(C // LANES):
            lanes = slice(c * LANES, (c + 1) * LANES)
            win = hs_ref[pl.ds(r0, span), lanes]
            acc = jnp.zeros((CONV_ROWS, LANES), F32)
            for phase in range(8):
                taps = [j for j in range(width) if (first_tap + j) % 8 == phase]
                if not taps:
                    continue
                shifted = win if phase == 0 else pltpu.roll(win, span - phase, axis=0)
                for j in taps:
                    a8 = first_tap + j - phase
                    acc = acc + w_ref[j:j + 1, lanes] * shifted[a8:a8 + CONV_ROWS]
            y_ref[pl.ds(r0, CONV_ROWS), lanes] = acc + cb_ref[:, lanes]
        return carry

    lax.fori_loop(0, tt // CONV_ROWS, rows, 0)

    y = _layer_norm(y_ref[...], g_ref[...], beta_ref[...])
    o_ref[...] = (y * jax.nn.sigmoid(y)).astype(o_ref.dtype)


def _conv_module(z, conv_dw, conv_b, ln_g, ln_b, seqs, *, tt):
    T = z.shape[0]
    width, C = conv_dw.shape
    assert width // 2 <= CONV_HALO and tt % CONV_ROWS == 0 and tt % CONV_HALO == 0
    assert 7 + ((CONV_HALO - width // 2 + width - 1) // 8) * 8 + CONV_ROWS <= CONV_ROWS + 2 * CONV_HALO
    halo_per_tile = tt // CONV_HALO
    last_halo = T // CONV_HALO - 1

    def cur(col):
        return pl.BlockSpec((tt, C), lambda i: (i, col))

    def prev(col):
        return pl.BlockSpec((CONV_HALO, C), lambda i: (jnp.maximum(i * halo_per_tile - 1, 0), col))

    def nxt(col):
        return pl.BlockSpec((CONV_HALO, C),
                            lambda i: (jnp.minimum((i + 1) * halo_per_tile, last_halo), col))

    def vec():
        return pl.BlockSpec((1, C), lambda i: (0, 0))

    return pl.pallas_call(
        functools.partial(_conv_kernel, tt=tt, seqs=seqs),
        grid=(T // tt,),
        in_specs=[cur(0), cur(1), prev(0), prev(1), nxt(0), nxt(1),
                  pl.BlockSpec((width, C), lambda i: (0, 0)), vec(), vec(), vec()],
        out_specs=pl.BlockSpec((tt, C), lambda i: (i, 0)),
        out_shape=jax.ShapeDtypeStruct((T, C), BF16),
        scratch_shapes=[pltpu.VMEM((tt + 2 * CONV_HALO, C), F32), pltpu.VMEM((tt, C), F32)],
        compiler_params=_params("parallel"),
        name="conv_module",
    )(z, z, z, z, z, z, conv_dw, conv_b.reshape(1, C), ln_g.reshape(1, C), ln_b.reshape(1, C))


def _sgu_kernel(u_ref, v_ref, g_ref, beta_ref, w_ref, bias_ref, o_ref, vn_ref):
    groups, chunk, _ = w_ref.shape
    tt, C = v_ref.shape
    gd = C // groups
    vn_ref[...] = _layer_norm(v_ref[...].astype(F32), g_ref[...], beta_ref[...]).astype(vn_ref.dtype)
    for n in range(tt // chunk):
        rows = slice(n * chunk, (n + 1) * chunk)
        for g in range(groups):
            lanes = slice(g * gd, (g + 1) * gd)
            s = _dot(w_ref[g], vn_ref[rows, lanes]) + bias_ref[g]
            o_ref[rows, lanes] = (u_ref[rows, lanes].astype(F32) * s).astype(o_ref.dtype)


def _sgu(z, ln_g, ln_b, sgu_w, sgu_bias, u_col, v_col, *, tt):
    T = z.shape[0]
    groups, chunk, _ = sgu_w.shape
    C = ln_g.shape[0]
    gd = C // groups
    assert gd % LANES == 0 and tt % chunk == 0

    def vec():
        return pl.BlockSpec((1, C), lambda i: (0, 0))

    return pl.pallas_call(
        _sgu_kernel,
        grid=(T // tt,),
        in_specs=[pl.BlockSpec((tt, C), lambda i: (i, u_col)),
                  pl.BlockSpec((tt, C), lambda i: (i, v_col)),
                  vec(), vec(),
                  pl.BlockSpec((groups, chunk, chunk), lambda i: (0, 0, 0)),
                  pl.BlockSpec((groups, chunk, gd), lambda i: (0, 0, 0))],
        out_specs=pl.BlockSpec((tt, C), lambda i: (i, 0)),
        out_shape=jax.ShapeDtypeStruct((T, C), BF16),
        scratch_shapes=[pltpu.VMEM((tt, C), BF16)],
        compiler_params=_params("parallel"),
        name="sgu",
    )(z, z, ln_g.reshape(1, C), ln_b.reshape(1, C), sgu_w, sgu_bias)


def _na_bias_table(rpb, na_cols):
    H, n_dr, _ = rpb.shape
    c = jnp.arange(GRID_W)
    c_start = jnp.clip(c - na_cols // 2, 0, GRID_W - na_cols)
    kc = jnp.arange(GRID_W)
    valid = (kc[None, :] >= c_start[:, None]) & (kc[None, :] < c_start[:, None] + na_cols)
    dc = jnp.clip(kc[None, :] - c[:, None] + (na_cols - 1), 0, 2 * na_cols - 2)
    dense = jnp.where(valid[None, None], rpb[:, :, dc].astype(F32), MASKED)
    pairs = jnp.concatenate([dense[:, :-1], dense[:, 1:]], axis=-1)
    pairs = pairs.reshape(H // 2, 2, n_dr - 1, GRID_W, 2 * GRID_W).transpose(0, 2, 1, 3, 4)
    return pairs.reshape(H // 2, n_dr - 1, 2 * GRID_W, 2 * GRID_W)


def _na_kernel(q_ref, kp_ref, kc_ref, kn_ref, vp_ref, vc_ref, vn_ref, tab_ref, o_ref,
               kbuf, vbuf, *, seqs, kr, scale):
    halo = NA_HALO_ROWS * GRID_W
    blk = NA_BLOCK_ROWS * GRID_W
    for buf, p, c, n in ((kbuf, kp_ref, kc_ref, kn_ref), (vbuf, vp_ref, vc_ref, vn_ref)):
        buf[0:halo, :] = p[...]
        buf[halo:halo + blk, :] = c[...]
        buf[halo + blk:, :] = n[...]

    n_pairs = q_ref.shape[1] // LANES
    row_seqs = tuple((n, length // GRID_W) for n, length in seqs)
    loc0, seq_rows = _seq_position(pl.program_id(0) * NA_BLOCK_ROWS, row_seqs)
    lane = lax.broadcasted_iota(jnp.int32, (GRID_W, LANES), 1)
    low = lane < GRID_W

    pair_lanes = [slice(hp * LANES, (hp + 1) * LANES) for hp in range(n_pairs)]

    def geometry(j):
        r_loc = loc0 + j
        r_start = jnp.clip(r_loc - kr // 2, 0, seq_rows - kr)
        dr0 = r_start - r_loc + (kr - 1)
        koff = pl.multiple_of((r_start - loc0 + NA_HALO_ROWS) * GRID_W, GRID_W)
        qoff = pl.multiple_of(j * GRID_W, GRID_W)
        return qoff, koff, dr0

    def score_pass(qoff, koff):
        scores = []
        for lanes in pair_lanes:
            q2 = q_ref[pl.ds(qoff, GRID_W), lanes].astype(F32) * scale
            qq = jnp.concatenate([jnp.where(low, q2, 0.0), jnp.where(low, 0.0, q2)], axis=0)
            scores.append(_dot_nt(qq.astype(BF16), kbuf[pl.ds(koff, kr * GRID_W), lanes]))
        return scores

    def softmax_pass(scores, dr0):
        probs = []
        for hp, s in enumerate(scores):
            s = s + jnp.concatenate([tab_ref[hp, dr0 + 2 * t] for t in range(kr // 2)], axis=1)
            probs.append(jnp.exp(s - jnp.max(s, axis=-1, keepdims=True)).astype(BF16))
        return probs

    ones = jnp.ones((kr * GRID_W, LANES), BF16)

    def value_pass(probs, qoff, koff):
        for lanes, p in zip(pair_lanes, probs):
            values = jnp.concatenate([vbuf[pl.ds(koff, kr * GRID_W), lanes], ones], axis=1)
            r = _dot(p, values)
            r = r[:, :LANES] / r[:, LANES:]
            o_ref[pl.ds(qoff, GRID_W), lanes] = jnp.where(low, r[:GRID_W], r[GRID_W:]).astype(o_ref.dtype)

    def rows(jj, carry):
        geo = [geometry(jj * NA_ROWS_PER_ITER + r) for r in range(NA_ROWS_PER_ITER)]
        scores = [score_pass(qoff, koff) for qoff, koff, _ in geo]
        for (qoff, koff, dr0), sc in zip(geo, scores):
            value_pass(softmax_pass(sc, dr0), qoff, koff)
        return carry

    lax.fori_loop(0, NA_BLOCK_ROWS // NA_ROWS_PER_ITER, rows, 0)


def _neighbourhood_attention(z, table, seqs, q_col, k_col, v_col, width, head_dim):
    T = z.shape[0]
    kr = table.shape[1] // 2 + 1
    assert head_dim * 2 == LANES and GRID_W * 2 == LANES and kr % 2 == 0
    assert kr == NA_BLOCK_ROWS and NA_HALO_ROWS >= kr // 2
    for _, length in seqs:
        assert length % (NA_BLOCK_ROWS * GRID_W) == 0 and length // GRID_W >= kr
    blk = NA_BLOCK_ROWS * GRID_W
    halo = NA_HALO_ROWS * GRID_W
    halo_per_blk = blk // halo
    last_halo = T // halo - 1

    def cur(col):
        return pl.BlockSpec((blk, width), lambda i: (i, col))

    def prev(col):
        return pl.BlockSpec((halo, width), lambda i: (jnp.maximum(i * halo_per_blk - 1, 0), col))

    def nxt(col):
        return pl.BlockSpec((halo, width),
                            lambda i: (jnp.minimum((i + 1) * halo_per_blk, last_halo), col))

    return pl.pallas_call(
        functools.partial(_na_kernel, seqs=seqs, kr=kr, scale=head_dim ** -0.5),
        grid=(T // blk,),
        in_specs=[cur(q_col), prev(k_col), cur(k_col), nxt(k_col),
                  prev(v_col), cur(v_col), nxt(v_col),
                  pl.BlockSpec(table.shape, lambda i: (0, 0, 0, 0))],
        out_specs=pl.BlockSpec((blk, width), lambda i: (i, 0)),
        out_shape=jax.ShapeDtypeStruct((T, width), BF16),
        scratch_shapes=[pltpu.VMEM((blk + 2 * halo, width), BF16),
                        pltpu.VMEM((blk + 2 * halo, width), BF16)],
        compiler_params=_params("parallel"),
        name="neighbourhood_attention",
    )(z, z, z, z, z, z, z, table)


def _xattn_kernel(q_ref, k_ref, v_ref, o_ref, *, scale):
    hd = q_ref.shape[1] // XA_HEADS
    head_lanes = [slice(h * hd, (h + 1) * hd) for h in range(XA_HEADS)]
    scores = [_dot_nt(q_ref[:, lanes], k_ref[0, :, lanes]) for lanes in head_lanes]
    probs, inv_sums = [], []
    for s in scores:
        s = s * scale
        p = jnp.exp(s - jnp.max(s, axis=-1, keepdims=True))
        inv_sums.append(1.0 / jnp.sum(p, axis=-1, keepdims=True))
        probs.append(p.astype(BF16))
    for lanes, p, inv in zip(head_lanes, probs, inv_sums):
        o_ref[:, lanes] = (_dot(p, v_ref[0, :, lanes]) * inv).astype(o_ref.dtype)


def _cross_attention(q, kv, seqs, *, tm):
    T, D = q.shape
    n_mem = kv.shape[1]
    (n_a, len_a), (_, len_b) = seqs
    assert len_a % tm == 0 and len_b % tm == 0 and (D // XA_HEADS) % LANES == 0
    split = n_a * len_a

    def seq_of(i):
        t0 = i * tm
        return jnp.where(t0 < split, t0 // len_a, n_a + (t0 - split) // len_b)

    return pl.pallas_call(
        functools.partial(_xattn_kernel, scale=(D // XA_HEADS) ** -0.5),
        grid=(T // tm,),
        in_specs=[pl.BlockSpec((tm, D), lambda i: (i, 0)),
                  pl.BlockSpec((1, n_mem, D), lambda i: (seq_of(i), 0, 0)),
                  pl.BlockSpec((1, n_mem, D), lambda i: (seq_of(i), 0, 1))],
        out_specs=pl.BlockSpec((tm, D), lambda i: (i, 0)),
        out_shape=jax.ShapeDtypeStruct((T, D), BF16),
        compiler_params=_params("parallel"),
        name="cross_attention",
    )(q, kv, kv)


def _ffn_kernel(x_ref, g_ref, w1_ref, w2_ref, o_ref, n_ref):
    @pl.when(pl.program_id(1) == 0)
    def _():
        x = x_ref[...]
        n_ref[...] = _rms(x, g_ref[...]).astype(n_ref.dtype)
        o_ref[...] = x

    h = jnp.maximum(_dot(n_ref[...], w1_ref[...]), 0.0)
    o_ref[...] += _dot((h * h).astype(BF16), w2_ref[...])


def _ffn(x, g, w1, w2, layer, *, tm, tf):
    T, D = x.shape
    F = w1.shape[2]
    return pl.pallas_call(
        _ffn_kernel,
        grid=(T // tm, F // tf),
        in_specs=[pl.BlockSpec((tm, D), lambda i, f: (i, 0)),
                  pl.BlockSpec((1, D), lambda i, f: (0, 0)),
                  _layer_weight(layer, (D, tf), lambda i, f: (0, f)),
                  _layer_weight(layer, (tf, D), lambda i, f: (f, 0))],
        out_specs=pl.BlockSpec((tm, D), lambda i, f: (i, 0)),
        out_shape=jax.ShapeDtypeStruct((T, D), F32),
        scratch_shapes=[pltpu.VMEM((tm, D), BF16)],
        compiler_params=_params("parallel", "arbitrary"),
        name="ffn",
    )(x, g.reshape(1, D), w1, w2)


def _rmsnorm_kernel(x_ref, g_ref, o_ref):
    o_ref[...] = _rms(x_ref[...], g_ref[...])


def _rmsnorm(x, g, row0, rows, *, tm):
    D = x.shape[1]
    assert row0 % tm == 0 and rows % tm == 0
    first = row0 // tm
    return pl.pallas_call(
        _rmsnorm_kernel,
        grid=(rows // tm,),
        in_specs=[pl.BlockSpec((tm, D), lambda i: (first + i, 0)), pl.BlockSpec((1, D), lambda i: (0, 0))],
        out_specs=pl.BlockSpec((tm, D), lambda i: (i, 0)),
        out_shape=jax.ShapeDtypeStruct((rows, D), F32),
        compiler_params=_params("parallel"),
        name="final_rmsnorm",
    )(x, g.reshape(1, D))


def _tiles(T, seqs, sgu_chunk):
    seq_gcd = math.gcd(*[length for _, length in seqs])
    group_gcd = math.gcd(*[n * length for n, length in seqs])
    return dict(
        square=_tile(group_gcd, 512),
        merge=_tile(T, 1024),
        final=_tile(seq_gcd, 1024),
        conv=_tile(seq_gcd, 512, CONV_ROWS),
        sgu=_tile(T, 1024, sgu_chunk),
        xattn=_tile(seq_gcd, 1024),
        ffn=_tile(T, 512),
    )


def kernel(x_prompt, x_sample, mem_prompt, mem_sample, norm_mix, w_in, conv_dw, conv_b, conv_ln_g, conv_ln_b, sgu_ln_g, sgu_ln_b, sgu_w, sgu_b, na_rpb, w_br_conv, w_br_sgu, w_br_na, w_out, norm_xa, norm_mem, xa_wq, xa_wk, xa_wv, xa_wo, norm_ffn, ffn_w1, ffn_w2, norm_final):
    depth, D, d_in = w_in.shape
    Bp, Sp, _ = x_prompt.shape
    Bs, Ss, _ = x_sample.shape
    seqs = ((Bp, Sp), (Bs, Ss))
    T = Bp * Sp + Bs * Ss
    d_conv = conv_dw.shape[-1]
    d_sgu = sgu_ln_g.shape[-1]
    d_na = w_br_na.shape[1]
    na_heads = na_rpb.shape[1]
    na_cols = (na_rpb.shape[3] + 1) // 2
    d_ff = ffn_w1.shape[-1]
    n_mem = mem_prompt.shape[1]
    assert d_conv == d_sgu == d_na, "column blocks of the input projection share one width"
    width = d_conv
    gate_col0 = 7 * width
    assert d_in == gate_col0 + 3 * D and width % LANES == 0

    tiles = _tiles(T, seqs, sgu_w.shape[2])
    tn_d = _tile(D, 1024)
    assert gate_col0 % tn_d == 0

    x_parts = (x_prompt.reshape(Bp * Sp, D), x_sample.reshape(Bs * Ss, D))
    mem =jnp.concatenate([mem_prompt.reshape(Bp * n_mem, D), mem_sample.reshape(Bs * n_mem, D)], axis=0)

    w_in, w_br_conv, w_br_sgu, w_br_na, w_out, xa_wq, xa_wo, ffn_w1, ffn_w2 = (
        w.astype(BF16) for w in (w_in, w_br_conv, w_br_sgu, w_br_na, w_out, xa_wq, xa_wo, ffn_w1, ffn_w2))
    w_kv = jnp.concatenate([xa_wk, xa_wv], axis=2).astype(BF16)

    for l in range(depth):
        z = _norm_matmul(x_parts, norm_mix[l], w_in, l)
        h_conv = _conv_module(z, conv_dw[l], conv_b[l], conv_ln_g[l], conv_ln_b[l], seqs, tt=tiles["conv"])
        sgu_bias = jnp.broadcast_to(sgu_b[l][:, :, None], sgu_b.shape[1:] + (d_sgu // sgu_w.shape[1],))
        h_sgu = _sgu(z, sgu_ln_g[l], sgu_ln_b[l], sgu_w[l].astype(BF16), sgu_bias, 2, 3, tt=tiles["sgu"])
        h_na = _neighbourhood_attention(z, _na_bias_table(na_rpb[l], na_cols), seqs, 4, 5, 6,
                                        width, d_na // na_heads)
        m = _merge(h_conv, h_sgu, h_na, w_br_conv, w_br_sgu, w_br_na, l, z, gate_col0,
                   tm=tiles["merge"], tn=tn_d)
        x = _matmul_residual(m, w_out, l, x_parts, tm=tiles["square"], tn=D)
        q = _norm_matmul((x,), norm_xa[l], xa_wq, l)
        kv = _norm_matmul((mem,), norm_mem[l], w_kv, l).reshape(Bp + Bs, n_mem, 2 * D)
        o = _cross_attention(q, kv, seqs, tm=tiles["xattn"])
        x = _matmul_residual(o, xa_wo, l, (x,), tm=tiles["square"], tn=D)
        x = _ffn(x, norm_ffn[l], ffn_w1, ffn_w2, l, tm=tiles["ffn"], tf=_tile(d_ff, 1024))
        x_parts = (x,)

    y_prompt = _rmsnorm(x, norm_final, 0, Bp * Sp, tm=tiles["final"]).reshape(Bp, Sp, D)
    y_sample = _rmsnorm(x, norm_final, Bp * Sp, Bs * Ss, tm=tiles["final"]).reshape(Bs, Ss, D)
    return (y_prompt, y_sample)
```
